```python
import jax, jax.numpy as jnp
from jax import lax
import numpy as np

D_MODEL = 4096
BATCH = 4
SEQ = 2048
DEPTH = 1
DEC_BATCH = 128
DEC_SEQ = 1
PAST_LEN = 2048
PAGE_SIZE = 128

D_RNN = D_MODEL
N_RNN_BLOCKS = 16
RNN_BLOCK = D_RNN // N_RNN_BLOCKS
CONV_W = 4
LRU_C = 8.0
HEAD_DIM = 128
HEADS_PER_GROUP = 8
DIL_GROUPS = ((128, 1), (512, 4), (2048, 16))
N_GROUPS = len(DIL_GROUPS)
N_HEADS = N_GROUPS * HEADS_PER_GROUP
D_ATT = N_HEADS * HEAD_DIM
D_ATT_OUT = HEADS_PER_GROUP * HEAD_DIM
ATT_SCALE = HEAD_DIM ** -0.5
N_EXPERT_GROUPS = 4
EXPERTS_PER_GROUP = 8
N_EXPERTS = N_EXPERT_GROUPS * EXPERTS_PER_GROUP
TOP_K = 2
D_EXPERT = 1024
MOE_BLOCK = 128
EPS = 1e-6
D_IN = 2 * D_RNN + 3 * D_ATT + 2 * D_MODEL
IN_SPLITS = (D_RNN, 2 * D_RNN, 2 * D_RNN + D_ATT, 2 * D_RNN + 2 * D_ATT, 2 * D_RNN + 3 * D_ATT, 2 * D_RNN + 3 * D_ATT + D_MODEL)

kernel_name = "hawk_dilated_hmoe_step"


def rms_norm(x, g):
    xf = x.astype(jnp.float32)
    y = xf * lax.rsqrt(jnp.mean(xf * xf, axis=-1, keepdims=True) + EPS)
    return (y * g.astype(jnp.float32)).astype(x.dtype)


def alibi_slopes():
    return 2.0 ** (-8.0 * jnp.arange(1, N_HEADS + 1, dtype=jnp.float32) / N_HEADS)


def causal_conv(x, buf, w, b):
    xe = jnp.concatenate([buf, x], axis=1)
    y = lax.conv_general_dilated(xe, w[:, None, :].astype(xe.dtype), window_strides=(1,), padding='VALID',
                                 dimension_numbers=('NWC', 'WIO', 'NWC'), feature_group_count=x.shape[-1])
    return y + b, xe[:, -(CONV_W - 1):]


def rglru(x, h0, reset, w_a, b_a, w_x, b_x, lam):
    B, T, C = x.shape
    xb = x.reshape(B, T, N_RNN_BLOCKS, RNN_BLOCK)
    r = jax.nn.sigmoid(jnp.einsum('btni,nij->btnj', xb, w_a).reshape(B, T, C) + b_a)
    i = jax.nn.sigmoid(jnp.einsum('btni,nij->btnj', xb, w_x).reshape(B, T, C) + b_x)
    log_a = -LRU_C * r.astype(jnp.float32) * jax.nn.softplus(-lam.astype(jnp.float32))
    rs = reset[None, :, None]
    a = jnp.where(rs, 0.0, jnp.exp(log_a))
    mult = jnp.where(rs, 1.0, jnp.sqrt(-jnp.expm1(2.0 * log_a)))
    u = mult * (i * x).astype(jnp.float32)

    def step(h, au):
        a_t, u_t = au
        h = a_t * h + u_t
        return h, h

    h_last, hs = lax.scan(step, h0.astype(jnp.float32), (jnp.swapaxes(a, 0, 1), jnp.swapaxes(u, 0, 1)))
    return jnp.swapaxes(hs, 0, 1).astype(x.dtype), h_last.astype(h0.dtype)


def dilated_attention_prompt(q, k, v, dil, span, slopes):
    B, S, H, E = q.shape
    L = S // dil
    n_blk = -(-L // span)
    Lp = n_blk * span

    def to_classes(t):
        t = t.reshape(B, L, dil, H, E).transpose(0, 2, 1, 3, 4)
        return jnp.pad(t, ((0, 0), (0, 0), (0, Lp - L), (0, 0), (0, 0)))

    def band(t):
        t = jnp.pad(t, ((0, 0), (0, 0), (span, 0), (0, 0), (0, 0))).reshape(B, dil, n_blk + 1, span, H, E)
        return jnp.concatenate([t[:, :, :-1], t[:, :, 1:]], axis=3)

    qb = to_classes(q).reshape(B, dil, n_blk, span, H, E)
    kb = band(to_classes(k))
    vb = band(to_classes(v))
    s = jnp.einsum('bdnqhe,bdnkhe->bdnhqk', qb, kb, preferred_element_type=jnp.float32)
    qi = jnp.arange(span)[:, None]
    kj = jnp.arange(2 * span)[None, :]
    diff = span + qi - kj
    key_idx = jnp.arange(n_blk)[:, None, None] * span + kj[None] - span
    valid = (diff >= 0) & (diff <= span) & (key_idx >= 0)
    s = s - slopes[:, None, None] * (diff * dil).astype(jnp.float32)
    s = jnp.where(valid[:, None], s, -jnp.inf)
    m = jnp.max(s, axis=-1, keepdims=True)
    p = jnp.exp(s - m)
    l = jnp.sum(p, axis=-1, keepdims=True)
    o = jnp.einsum('bdnhqk,bdnkhe->bdnqhe', p / l, vb.astype(jnp.float32))
    lse = (m + jnp.log(l))[..., 0]
    o = o.reshape(B, dil, Lp, H, E)[:, :, :L].transpose(0, 2, 1, 3, 4).reshape(B, S, H, E)
    lse = lse.transpose(0, 1, 2, 4, 3).reshape(B, dil, Lp, H)[:, :, :L].transpose(0, 2, 1, 3).reshape(B, S, H)
    return o, lse


def dilated_attention_decode(q, k, v, kv_buf, dil, span, slopes):
    N, T, H, E = q.shape
    Wb = kv_buf.shape[1]
    kv_all = jnp.concatenate([kv_buf, jnp.stack([k, v], axis=2)], axis=1)
    steps = jnp.arange(span + 1)
    idx = Wb + jnp.arange(T)[:, None] - steps[None, :] * dil
    valid = idx >= 0
    kv_g = jnp.take(kv_all, jnp.maximum(idx, 0), axis=1)
    s = jnp.einsum('nthe,ntkhe->nhtk', q, kv_g[:, :, :, 0], preferred_element_type=jnp.float32)
    s = s - slopes[:, None, None] * (steps * dil).astype(jnp.float32)
    s = jnp.where(valid, s, -jnp.inf)
    m = jnp.max(s, axis=-1, keepdims=True)
    p = jnp.exp(s - m)
    l = jnp.sum(p, axis=-1, keepdims=True)
    o = jnp.einsum('nhtk,ntkhe->nthe', p / l, kv_g[:, :, :, 1].astype(jnp.float32))
    lse = jnp.transpose((m + jnp.log(l))[..., 0], (0, 2, 1))
    return o, lse, kv_all[:, -Wb:]


def hierarchical_route(xn, w_rg, b_rg, w_re, b_re):
    N = xn.shape[0]
    rows = jnp.arange(N)
    lg = (xn @ w_rg).astype(jnp.float32) + b_rg
    g_sel = jnp.argmax(lg, axis=-1)
    p_group = jax.nn.softmax(lg, axis=-1)[rows, g_sel]
    le = ((xn @ w_re).astype(jnp.float32) + b_re).reshape(N, N_EXPERT_GROUPS, EXPERTS_PER_GROUP)
    top_v, top_i = lax.top_k(le[rows, g_sel], TOP_K)
    gate = jax.nn.softmax(top_v, axis=-1) * p_group[:, None]
    eid = (g_sel[:, None] * EXPERTS_PER_GROUP + top_i).astype(jnp.int32)
    return eid, gate


def moe_ffn(xn, eid, gate, w_gate, w_up, w_down):
    N, D = xn.shape
    A = N * TOP_K
    e_flat = eid.reshape(A)
    tok = jnp.repeat(jnp.arange(N, dtype=jnp.int32), TOP_K)
    g_flat = gate.reshape(A)
    order = jnp.argsort(e_flat)
    e_sorted = e_flat[order]
    counts = jnp.bincount(e_flat, length=N_EXPERTS)
    padded = (counts + MOE_BLOCK - 1) // MOE_BLOCK * MOE_BLOCK
    start = jnp.cumsum(counts) - counts
    p_end = jnp.cumsum(padded)
    p_start = p_end - padded
    dest = p_start[e_sorted] + jnp.arange(A) - start[e_sorted]
    n_blocks = -(-A // MOE_BLOCK) + N_EXPERTS
    n_rows = n_blocks * MOE_BLOCK
    row_tok = jnp.full((n_rows,), N, jnp.int32).at[dest].set(tok[order])
    row_gate = jnp.zeros((n_rows,), jnp.float32).at[dest].set(g_flat[order].astype(jnp.float32))
    blk_exp = jnp.minimum(jnp.searchsorted(p_end, jnp.arange(n_blocks) * MOE_BLOCK, side='right'), N_EXPERTS - 1)
    x_pad = jnp.concatenate([xn, jnp.zeros((1, D), xn.dtype)], axis=0)

    def expert_block(args):
        rows, e = args
        xb = x_pad[rows]
        hb = jax.nn.silu(xb @ w_gate[e]) * (xb @ w_up[e])
        return hb @ w_down[e]

    yb = lax.map(expert_block, (row_tok.reshape(n_blocks, MOE_BLOCK), blk_exp))
    y = yb.reshape(n_rows, D).astype(jnp.float32) * row_gate[:, None]
    return jax.ops.segment_sum(y, row_tok, num_segments=N + 1)[:N].astype(xn.dtype)


def trunk_layer(x, conv_buf, h0, kv_bufs, positions, lw):
    (norm1, w_in, conv_w, conv_b, lru_wa, lru_ba, lru_wx, lru_bx, lru_lam, w_br_a, w_br_b, w_out,
     norm2, rg_w, rg_b, re_w, re_b, e_gate, e_up, e_down) = lw
    B, T, _ = x.shape
    xn = rms_norm(x, norm1)
    xr, yr, q, k, v, ga, gb = jnp.split(xn @ w_in, IN_SPLITS, axis=-1)
    xc, new_conv = causal_conv(xr, conv_buf, conv_w, conv_b)
    hs, new_h = rglru(xc, h0, positions == 0, lru_wa, lru_ba, lru_wx, lru_bx, lru_lam)
    y_a = (jax.nn.gelu(yr) * hs) @ w_br_a
    q = q.reshape(B, T, N_GROUPS, HEADS_PER_GROUP, HEAD_DIM) * ATT_SCALE
    k = k.reshape(B, T, N_GROUPS, HEADS_PER_GROUP, HEAD_DIM)
    v = v.reshape(B, T, N_GROUPS, HEADS_PER_GROUP, HEAD_DIM)
    slopes = alibi_slopes()
    outs, lses, new_kv = [], [], []
    for g, (win, dil) in enumerate(DIL_GROUPS):
        span = win // dil
        sl = slopes[g * HEADS_PER_GROUP:(g + 1) * HEADS_PER_GROUP]
        qg, kg, vg = q[:, :, g], k[:, :, g], v[:, :, g]
        if kv_bufs is None:
            o, lse = dilated_attention_prompt(qg, kg, vg, dil, span, sl)
            kv = jnp.stack([kg, vg], axis=2)[:, -min(win, T):]
        else:
            o, lse, kv = dilated_attention_decode(qg, kg, vg, kv_bufs[g], dil, span, sl)
        outs.append(o)
        lses.append(lse)
        new_kv.append(kv)
    alpha = jax.nn.softmax(jnp.stack(lses, axis=0), axis=0)
    o = jnp.einsum('gbth,gbthe->bthe', alpha, jnp.stack(outs, axis=0))
    y_b = o.reshape(B, T, D_ATT_OUT).astype(x.dtype) @ w_br_b
    u = jax.nn.sigmoid(ga) * y_a + jax.nn.sigmoid(gb) * y_b
    h = x + u @ w_out
    hn = rms_norm(h, norm2).reshape(B * T, D_MODEL)
    eid, gate = hierarchical_route(hn, rg_w, rg_b, re_w, re_b)
    y = h + moe_ffn(hn, eid, gate, e_gate, e_up, e_down).reshape(B, T, D_MODEL)
    return y, new_kv, new_conv, new_h


def setup_inputs(seed: int = 0) -> dict:
    key = jax.random.key(seed)
    ks = jax.random.split(key, 32)
    f32 = jnp.float32

    def nrm(k, shape, scale):
        return jax.random.normal(k, shape, f32) * scale

    wb = [min(w, PAST_LEN) for w, _ in DIL_GROUPS]
    u = jax.random.uniform(ks[12], (DEPTH, D_RNN), f32, 0.9, 0.999)
    s = u ** (1.0 / LRU_C)
    return {
        "x_prompt": nrm(ks[0], (BATCH, SEQ, D_MODEL), 1.0),
        "x_sample": nrm(ks[1], (DEC_BATCH, DEC_SEQ, D_MODEL), 1.0),
        "cache_kv0": nrm(ks[2], (DEPTH, DEC_BATCH, wb[0], 2, HEADS_PER_GROUP, HEAD_DIM), 1.0),
        "cache_kv1": nrm(ks[3], (DEPTH, DEC_BATCH, wb[1], 2, HEADS_PER_GROUP, HEAD_DIM), 1.0),
        "cache_kv2": nrm(ks[4], (DEPTH, DEC_BATCH, wb[2], 2, HEADS_PER_GROUP, HEAD_DIM), 1.0),
        "state_conv": nrm(ks[5], (DEPTH, DEC_BATCH, CONV_W - 1, D_RNN), 1.0),
        "state_rglru": nrm(ks[6], (DEPTH, DEC_BATCH, D_RNN), 0.5),
        "norm1": 1.0 + nrm(ks[7], (DEPTH, D_MODEL), 0.05),
        "w_in": nrm(ks[8], (DEPTH, D_MODEL, D_IN), D_MODEL ** -0.5),
        "conv_w": nrm(ks[9], (DEPTH, CONV_W, D_RNN), CONV_W ** -0.5),
        "conv_b": nrm(ks[10], (DEPTH, D_RNN), 0.01),
        "lru_wa": nrm(ks[11], (DEPTH, N_RNN_BLOCKS, RNN_BLOCK, RNN_BLOCK), RNN_BLOCK ** -0.5),
        "lru_ba": nrm(ks[13], (DEPTH, D_RNN), 0.01),
        "lru_wx": nrm(ks[14], (DEPTH, N_RNN_BLOCKS, RNN_BLOCK, RNN_BLOCK), RNN_BLOCK ** -0.5),
        "lru_bx": nrm(ks[15], (DEPTH, D_RNN), 0.01),
        "lru_lambda": jnp.log(s) - jnp.log1p(-s),
        "w_branch_a": nrm(ks[16], (DEPTH, D_RNN, D_MODEL), D_RNN ** -0.5),
        "w_branch_b": nrm(ks[17], (DEPTH, D_ATT_OUT, D_MODEL), D_ATT_OUT ** -0.5),
        "w_out": nrm(ks[18], (DEPTH, D_MODEL, D_MODEL), D_MODEL ** -0.5),
        "norm2": 1.0 + nrm(ks[19], (DEPTH, D_MODEL), 0.05),
        "router_group_w": nrm(ks[20], (DEPTH, D_MODEL, N_EXPERT_GROUPS), D_MODEL ** -0.5),
        "router_group_b": nrm(ks[21], (DEPTH, N_EXPERT_GROUPS), 0.01),
        "router_expert_w": nrm(ks[22], (DEPTH, D_MODEL, N_EXPERTS), D_MODEL ** -0.5),
        "router_expert_b": nrm(ks[23], (DEPTH, N_EXPERTS), 0.01),
        "expert_w_gate": nrm(ks[24], (DEPTH, N_EXPERTS, D_MODEL, D_EXPERT), D_MODEL ** -0.5),
        "expert_w_up": nrm(ks[25], (DEPTH, N_EXPERTS, D_MODEL, D_EXPERT), D_MODEL ** -0.5),
        "expert_w_down": nrm(ks[26], (DEPTH, N_EXPERTS, D_EXPERT, D_MODEL), D_EXPERT ** -0.5),
        "norm_final": 1.0 + nrm(ks[27], (D_MODEL,), 0.05),
    }


def reference(x_prompt, x_sample, cache_kv0, cache_kv1, cache_kv2, state_conv, state_rglru,
              norm1, w_in, conv_w, conv_b, lru_wa, lru_ba, lru_wx, lru_bx, lru_lambda,
              w_branch_a, w_branch_b, w_out, norm2, router_group_w, router_group_b,
              router_expert_w, router_expert_b, expert_w_gate, expert_w_up, expert_w_down, norm_final):
    B, S, _ = x_prompt.shape
    N, T, _ = x_sample.shape
    pos_prompt = jnp.arange(S)
    pos_sample = PAST_LEN + jnp.arange(T)
    yp, ys = x_prompt, x_sample
    p_kv = [[], [], []]
    s_kv = [[], [], []]
    p_conv, p_h, s_conv, s_h = [], [], [], []
    for l in range(DEPTH):
        lw = (norm1[l], w_in[l], conv_w[l], conv_b[l], lru_wa[l], lru_ba[l], lru_wx[l], lru_bx[l],
              lru_lambda[l], w_branch_a[l], w_branch_b[l], w_out[l], norm2[l], router_group_w[l],
              router_group_b[l], router_expert_w[l], router_expert_b[l], expert_w_gate[l],
              expert_w_up[l], expert_w_down[l])
        conv0 = jnp.zeros((B, CONV_W - 1, D_RNN), yp.dtype)
        h_init = jnp.zeros((B, D_RNN), yp.dtype)
        yp, kv_p, c_p, h_p = trunk_layer(yp, conv0, h_init, None, pos_prompt, lw)
        ys, kv_s, c_s, h_s = trunk_layer(ys, state_conv[l], state_rglru[l],
                                         (cache_kv0[l], cache_kv1[l], cache_kv2[l]), pos_sample, lw)
        for g in range(N_GROUPS):
            p_kv[g].append(kv_p[g])
            s_kv[g].append(kv_s[g])
        p_conv.append(c_p)
        p_h.append(h_p)
        s_conv.append(c_s)
        s_h.append(h_s)
    y_prompt = rms_norm(yp, norm_final)
    y_sample = rms_norm(ys, norm_final)
    p_kv0, p_kv1, p_kv2 = jnp.stack(p_kv[0]), jnp.stack(p_kv[1]), jnp.stack(p_kv[2])
    s_kv0, s_kv1, s_kv2 = jnp.stack(s_kv[0]), jnp.stack(s_kv[1]), jnp.stack(s_kv[2])
    p_conv_st, p_h_st = jnp.stack(p_conv), jnp.stack(p_h)
    s_conv_st, s_h_st = jnp.stack(s_conv), jnp.stack(s_h)
    return (y_prompt, y_sample, p_kv0, p_kv1, p_kv2, p_conv_st, p_h_st, s_kv0, s_kv1, s_kv2, s_conv_st, s_h_st)
```

```python
import functools

import jax
import jax.numpy as jnp
from jax import lax
from jax.experimental import pallas as pl
from jax.experimental.pallas import tpu as pltpu

F32, BF16, I32 = jnp.float32, jnp.bfloat16, jnp.int32

N_RNN_BLOCKS = 16
CONV_W = 4
LRU_C = 8.0
HEAD_DIM = 128
HEADS_PER_GROUP = 8
DIL_GROUPS = ((128, 1), (512, 4), (2048, 16))
N_GROUPS = len(DIL_GROUPS)
SPAN = 128
N_EXPERT_GROUPS = 4
EXPERTS_PER_GROUP = 8
N_EXPERTS = N_EXPERT_GROUPS * EXPERTS_PER_GROUP
TOP_K = 2
EPS = 1e-6
PAST_LEN = 2048
ATT_SCALE = HEAD_DIM ** -0.5
NEG = -1e30

LANES = 128
SUBLANES = 8
VMEM_LIMIT = 56 * 1024 * 1024
ROW_TILE = 1024
COL_TILE = 512
MOE_ROWS = 1024
MOE_GRAN = 128
MOE_HIDDEN_CHUNK = 256
SCAN_CHUNKS = 16
SCAN_PITCH_PAD = 8
CACHE_SPLIT = 8


def _cparams(*sem):
    return pltpu.CompilerParams(dimension_semantics=sem, vmem_limit_bytes=VMEM_LIMIT)


def _bf16_round(x):
    return x.astype(BF16).astype(F32)


def _sigmoid(x):
    return 1.0 / (1.0 + jnp.exp(-x))


def _gelu_tanh(x):
    return 0.5 * x * (1.0 + jnp.tanh(0.7978845608028654 * (x + 0.044715 * (x * x * x))))


def _one_minus_exp2x(x):
    t = jnp.tanh(x)
    return (-2.0 * t) / (1.0 - t)


def _softplus(x):
    return jnp.maximum(x, 0.0) + jnp.log1p(jnp.exp(-jnp.abs(x)))


def _dot(a, b):
    return jnp.dot(a, b, preferred_element_type=F32)


def _col_tile(n, *offsets):
    t = COL_TILE
    while t > LANES and (n % t or any(o % t for o in offsets)):
        t //= 2
    return t


def _row_tile(m, pref=ROW_TILE):
    best = None
    for t in range(LANES, min(m, pref) + 1, LANES):
        if m % t == 0:
            best = t
    return best or m


def _rmsnorm_kernel(x_ref, g_ref, o_ref):
    x = x_ref[...]
    y = x * lax.rsqrt(jnp.mean(x * x, axis=-1, keepdims=True) + EPS)
    o_ref[...] = (y * g_ref[...]).astype(o_ref.dtype)


def _rmsnorm(x, g, out_dtype):
    m, d = x.shape
    tm = _row_tile(m, 256)
    return pl.pallas_call(
        _rmsnorm_kernel,
        grid=(m // tm,),
        in_specs=[pl.BlockSpec((tm, d), lambda i: (i, 0)),
                  pl.BlockSpec((1, d), lambda i: (0, 0))],
        out_specs=pl.BlockSpec((tm, d), lambda i: (i, 0)),
        out_shape=jax.ShapeDtypeStruct((m, d), out_dtype),
        compiler_params=_cparams("parallel"),
        name="rmsnorm",
    )(x, g.reshape(1, d))


def _mm_kernel(x_ref, w_ref, o_ref, wb_ref):
    @pl.when(pl.program_id(1) == 0)
    def _():
        wb_ref[...] = w_ref[...].astype(BF16)

    o_ref[...] = _dot(x_ref[...], wb_ref[...]).astype(o_ref.dtype)


def _matmul(x, w, out_dtype):
    m, k = x.shape
    n = w.shape[1]
    tm = _row_tile(m)
    tn = _col_tile(n)
    return pl.pallas_call(
        _mm_kernel,
        grid=(n // tn, m // tm),
        in_specs=[pl.BlockSpec((tm, k), lambda j, i: (i, 0)),
                  pl.BlockSpec((k, tn), lambda j, i: (0, j))],
        out_specs=pl.BlockSpec((tm, tn), lambda j, i: (i, j)),
        out_shape=jax.ShapeDtypeStruct((m, n), out_dtype),
        scratch_shapes=[pltpu.VMEM((k, tn), BF16)],
        compiler_params=_cparams("parallel", "arbitrary"),
        name="matmul",
    )(x, w)


def _merge_kernel(a_ref, o_ref, wa_ref, wb_ref, ga_ref, gb_ref, u_ref, wab_ref, wbb_ref):
    @pl.when(pl.program_id(1) == 0)
    def _():
        wab_ref[...] = wa_ref[...].astype(BF16)
        wbb_ref[...] = wb_ref[...].astype(BF16)

    ya = _dot(a_ref[...], wab_ref[...])
    yb = _dot(o_ref[...], wbb_ref[...])
    u = _sigmoid(ga_ref[...]) * ya + _sigmoid(gb_ref[...]) * yb
    u_ref[...] = u.astype(u_ref.dtype)


def _merge(a, o, w_a, w_b, proj, ga_off, gb_off):
    m, ka = a.shape
    kb = o.shape[1]
    n = w_a.shape[1]
    tm = _row_tile(m)
    tn = _col_tile(n, ga_off, gb_off)
    return pl.pallas_call(
        _merge_kernel,
        grid=(n // tn, m // tm),
        in_specs=[pl.BlockSpec((tm, ka), lambda j, i: (i, 0)),
                  pl.BlockSpec((tm, kb), lambda j, i: (i, 0)),
                  pl.BlockSpec((ka, tn), lambda j, i: (0, j)),
                  pl.BlockSpec((kb, tn), lambda j, i: (0, j)),
                  pl.BlockSpec((tm, tn), lambda j, i: (i, j + ga_off // tn)),
                  pl.BlockSpec((tm, tn), lambda j, i: (i, j + gb_off // tn))],
        out_specs=pl.BlockSpec((tm, tn), lambda j, i: (i, j)),
        out_shape=jax.ShapeDtypeStruct((m, n), BF16),
        scratch_shapes=[pltpu.VMEM((ka, tn), BF16), pltpu.VMEM((kb, tn), BF16)],
        compiler_params=_cparams("parallel", "arbitrary"),
        name="merge",
    )(a, o, w_a, w_b, proj, proj)


def _outproj_kernel(x_ref, u_ref, w_ref, h_ref, wb_ref):
    @pl.when(pl.program_id(1) == 0)
    def _():
        wb_ref[...] = w_ref[...].astype(BF16)

    h_ref[...] = x_ref[...] + _dot(u_ref[...], wb_ref[...])


def _outproj(x, u, w):
    m, k = u.shape
    n = w.shape[1]
    tm = _row_tile(m)
    tn = _col_tile(n)
    return pl.pallas_call(
        _outproj_kernel,
        grid=(n // tn, m // tm),
        in_specs=[pl.BlockSpec((tm, tn), lambda j, i: (i, j)),
                  pl.BlockSpec((tm, k), lambda j, i: (i, 0)),
                  pl.BlockSpec((k, tn), lambda j, i: (0, j))],
        out_specs=pl.BlockSpec((tm, tn), lambda j, i: (i, j)),
        out_shape=jax.ShapeDtypeStruct((m, n), F32),
        scratch_shapes=[pltpu.VMEM((k, tn), BF16)],
        compiler_params=_cparams("parallel", "arbitrary"),
        name="outproj",
    )(x, u, w)


def _rglru_prompt_kernel(xr_ref, yr_ref, cw_ref, cb_ref, wa_ref, ba_ref, wx_ref, bx_ref, lam_ref,
                         a_out_ref, ph_ref,
                         xe_ref, a_ref, u_ref, hs_ref, as_ref, hsz_ref, *, t_len, nc):
    c = xr_ref.shape[1]
    nl = c // LANES
    cl = t_len // nc
    pitch = cl + SCAN_PITCH_PAD
    xe_ref[0:SUBLANES, :] = jnp.zeros((SUBLANES, c), F32)
    xe_ref[SUBLANES:SUBLANES + t_len, :] = _bf16_round(xr_ref[...])
    wa = wa_ref[0].astype(BF16)
    wx = wx_ref[0].astype(BF16)
    decay = (-LRU_C) * _softplus(-lam_ref[...])
    cw = _bf16_round(cw_ref[...])
    for ch in range(nc):
        r0 = ch * cl
        xc = cw[0:1] * xe_ref[r0 + 5:r0 + 5 + cl, :]
        for j in range(1, CONV_W):
            xc = xc + cw[j:j + 1] * xe_ref[r0 + 5 + j:r0 + 5 + j + cl, :]
        xc = xc + cb_ref[...]
        xcb = xc.astype(BF16)
        r = _sigmoid(_dot(xcb, wa) + ba_ref[...])
        i = _sigmoid(_dot(xcb, wx) + bx_ref[...])
        log_a = r * decay
        a = jnp.exp(log_a)
        mult = jnp.sqrt(_one_minus_exp2x(log_a))
        if ch == 0:
            row = lax.broadcasted_iota(I32, (cl, c), 0)
            a = jnp.where(row == 0, 0.0, a)
            mult = jnp.where(row == 0, 1.0, mult)
        au = (a, mult * (i * xc))
        for lb in range(nl):
            for ref, val in zip((a_ref, u_ref), au):
                ref[lb, ch * pitch:ch * pitch + cl, :] = val[:, lb * LANES:(lb + 1) * LANES]

    def pass1(j, carry):
        out = []
        for lb in range(nl):
            acc_a, acc_h = carry[2 * lb], carry[2 * lb + 1]
            a = a_ref[lb, pl.ds(j, nc, stride=pitch), :]
            u = u_ref[lb, pl.ds(j, nc, stride=pitch), :]
            acc_a = a * acc_a
            acc_h = a * acc_h + u
            as_ref[lb, j] = acc_a
            hsz_ref[lb, j] = acc_h
            out += [acc_a, acc_h]
        return tuple(out)

    init = (jnp.ones((nc, LANES), F32), jnp.zeros((nc, LANES), F32)) * nl
    fin = lax.fori_loop(0, cl, pass1, init)
    cins = []
    for lb in range(nl):
        fin_a, fin_h = fin[2 * lb], fin[2 * lb + 1]
        carry = jnp.zeros((1, LANES), F32)
        rows = []
        for ch in range(nc):
            rows.append(carry)
            carry = fin_a[ch:ch + 1] * carry + fin_h[ch:ch + 1]
        cins.append(jnp.concatenate(rows, axis=0))
        ph_ref[0, :, lb * LANES:(lb + 1) * LANES] = carry

    def pass2(j, _):
        for lb in range(nl):
            hs_ref[lb, pl.ds(j, nc, stride=pitch), :] = hsz_ref[lb, j] + as_ref[lb, j] * cins[lb]
        return 0

    lax.fori_loop(0, cl, pass2, 0)
    for ch in range(nc):
        for lb in range(nl):
            y = yr_ref[ch * cl:(ch + 1) * cl, lb * LANES:(lb + 1) * LANES]
            a_out_ref[ch * cl:(ch + 1) * cl, lb * LANES:(lb + 1) * LANES] = (
                _gelu_tanh(y) * hs_ref[lb, ch * pitch:ch * pitch + cl, :]).astype(a_out_ref.dtype)


def _rglru_prompt(proj, batch, t_len, d, conv_w, conv_b, lru_wa, lru_ba, lru_wx, lru_bx, lam):
    nb = lru_wa.shape[0]
    c = d // nb
    nl = c // LANES
    nc = SCAN_CHUNKS
    cl = t_len // nc
    pitch = cl + SCAN_PITCH_PAD
    vec = lambda v: v.reshape(1, d)
    vspec = pl.BlockSpec((1, c), lambda b, n: (0, n))
    wspec = pl.BlockSpec((1, c, c), lambda b, n: (n, 0, 0))
    return pl.pallas_call(
        functools.partial(_rglru_prompt_kernel, t_len=t_len, nc=nc),
        grid=(batch, nb),
        in_specs=[pl.BlockSpec((t_len, c), lambda b, n: (b, n)),
                  pl.BlockSpec((t_len, c), lambda b, n: (b, n + nb)),
                  pl.BlockSpec((CONV_W, c), lambda b, n: (0, n)),
                  vspec, wspec, vspec, wspec, vspec, vspec],
        out_specs=[pl.BlockSpec((t_len, c), lambda b, n: (b, n)),
                   pl.BlockSpec((1, 1, c), lambda b, n: (b, 0, n))],
        out_shape=[jax.ShapeDtypeStruct((batch * t_len, d), BF16),
                   jax.ShapeDtypeStruct((batch, 1, d), F32)],
        scratch_shapes=[pltpu.VMEM((t_len + SUBLANES, c), F32),
                        pltpu.VMEM((nl, nc * pitch, LANES), F32),
                        pltpu.VMEM((nl, nc * pitch, LANES), F32),
                        pltpu.VMEM((nl, nc * pitch, LANES), F32),
                        pltpu.VMEM((nl, cl, nc, LANES), F32),
                        pltpu.VMEM((nl, cl, nc, LANES), F32)],
        compiler_params=_cparams("parallel", "parallel"),
        name="rglru_prompt",
    )(proj, proj, conv_w, vec(conv_b), lru_wa, vec(lru_ba), lru_wx, vec(lru_bx), vec(lam))


def _rglru_step_kernel(xr_ref, yr_ref, cs_ref, h0_ref, cw_ref, cb_ref, wa_ref, ba_ref, wx_ref, bx_ref,
                       lam_ref, a_out_ref, ncs_ref, nh_ref, *, reset):
    x = xr_ref[...]
    cw = _bf16_round(cw_ref[...])
    xc = cw[CONV_W - 1:CONV_W] * _bf16_round(x)
    for j in range(CONV_W - 1):
        xc = xc + cw[j:j + 1] * _bf16_round(cs_ref[j])
    xc = xc + cb_ref[...]
    xcb = xc.astype(BF16)
    r = _sigmoid(_dot(xcb, wa_ref[0].astype(BF16)) + ba_ref[...])
    i = _sigmoid(_dot(xcb, wx_ref[0].astype(BF16)) + bx_ref[...])
    log_a = r * ((-LRU_C) * _softplus(-lam_ref[...]))
    if reset:
        h = i * xc
    else:
        h = jnp.exp(log_a) * h0_ref[...] + jnp.sqrt(_one_minus_exp2x(log_a)) * (i * xc)
    nh_ref[...] = h
    a_out_ref[...] = (_gelu_tanh(yr_ref[...]) * h).astype(a_out_ref.dtype)
    for j in range(CONV_W - 2):
        ncs_ref[j] = cs_ref[j + 1]
    ncs_ref[CONV_W - 2] = x


def _rglru_step(proj, row_off, n, conv_state, h0, d, conv_w, conv_b, lru_wa, lru_ba, lru_wx, lru_bx,
                lam):
    nb = lru_wa.shape[0]
    c = d // nb
    assert row_off % n == 0
    rb = row_off // n
    vec = lambda v: v.reshape(1, d)
    vspec = pl.BlockSpec((1, c), lambda j: (0, j))
    wspec = pl.BlockSpec((1, c, c), lambda j: (j, 0, 0))
    xspec = pl.BlockSpec((n, c), lambda j: (0, j))
    sspec = pl.BlockSpec((CONV_W - 1, n, c), lambda j: (0, 0, j))
    return pl.pallas_call(
        functools.partial(_rglru_step_kernel, reset=(PAST_LEN == 0)),
        grid=(nb,),
        in_specs=[pl.BlockSpec((n, c), lambda j: (rb, j)),
                  pl.BlockSpec((n, c), lambda j: (rb, j + nb)), sspec, xspec,
                  pl.BlockSpec((CONV_W, c), lambda j: (0, j)),
                  vspec, wspec, vspec, wspec, vspec, vspec],
        out_specs=[xspec, sspec, xspec],
        out_shape=[jax.ShapeDtypeStruct((n, d), BF16),
                   jax.ShapeDtypeStruct((CONV_W - 1, n, d), F32),
                   jax.ShapeDtypeStruct((n, d), F32)],
        compiler_params=_cparams("parallel"),
        name="rglru_step",
    )(proj, proj, conv_state, h0, conv_w, vec(conv_b), lru_wa, vec(lru_ba), lru_wx, vec(lru_bx),
      vec(lam))


def _merge_groups(lses, outs):
    m = functools.reduce(jnp.maximum, lses)
    es = [jnp.exp(x - m) for x in lses]
    den = functools.reduce(lambda a, b: a + b, es)
    acc = _bf16_round(es[0] / den) * _bf16_round(outs[0])
    for g in range(1, len(outs)):
        acc = acc + _bf16_round(es[g] / den) * _bf16_round(outs[g])
    return acc


def _attn_prompt_kernel(slope_ref, *refs, t_len):
    qkv = refs[:3 * N_GROUPS]
    o_ref = refs[3 * N_GROUPS]
    og_ref, lse_ref = refs[3 * N_GROUPS + 1:]
    h = pl.program_id(1)

    for g, (_, dil) in enumerate(DIL_GROUPS):
        q_ref, k_ref, v_ref = qkv[3 * g:3 * g + 3]
        slope = slope_ref[g * HEADS_PER_GROUP + h] * float(dil)
        n_blk = t_len // dil // SPAN

        def rows(ref, start, size, dil=dil):
            if dil == 1:
                return ref[pl.ds(start, size), :]
            return ref[pl.ds(start, size, stride=dil), :]

        def put(ref, g, start, val, dil=dil):
            if dil == 1:
                ref[g, pl.ds(start, SPAN), :] = val
            else:
                ref[g, pl.ds(start, SPAN, stride=dil), :] = val

        def block(base, with_prev, g=g, dil=dil, slope=slope, q_ref=q_ref, k_ref=k_ref,
                  v_ref=v_ref, rows=rows, put=put):
            nk = 2 * SPAN if with_prev else SPAN
            k0 = base - SPAN * dil if with_prev else base
            q = (rows(q_ref, base, SPAN) * ATT_SCALE).astype(BF16)
            kk = rows(k_ref, k0, nk).astype(BF16)
            vv = rows(v_ref, k0, nk).astype(BF16)
            s = lax.dot_general(q, kk, (((1,), (1,)), ((), ())), preferred_element_type=F32)
            qi = lax.broadcasted_iota(I32, (SPAN, nk), 0)
            kj = lax.broadcasted_iota(I32, (SPAN, nk), 1)
            diff = qi - kj + (SPAN if with_prev else 0)
            valid = (diff >= 0) & (diff <= SPAN)
            s = jnp.where(valid, s - slope * diff.astype(F32), NEG)
            m = jnp.max(s, axis=-1, keepdims=True)
            p = jnp.exp(s - m)
            l = jnp.sum(p, axis=-1, keepdims=True)
            put(og_ref, g, base, _dot((p / l).astype(BF16), vv))
            put(lse_ref, g, base, jnp.broadcast_to(m + jnp.log(l), (SPAN, LANES)))

        def per_class(r, _, block=block, n_blk=n_blk, dil=dil):
            block(r, False)

            def later(n, _):
                block(n * (SPAN * dil) + r, True)
                return 0

            if n_blk > 1:
                lax.fori_loop(1, n_blk, later, 0)
            return 0

        if dil == 1:
            per_class(0, 0)
        else:
            lax.fori_loop(0, dil, per_class, 0)

    rc = 256
    for c0 in range(0, t_len, rc):
        acc = _merge_groups([lse_ref[g, c0:c0 + rc, :] for g in range(N_GROUPS)],
                            [og_ref[g, c0:c0 + rc, :] for g in range(N_GROUPS)])
        o_ref[c0:c0 + rc, :] = acc.astype(o_ref.dtype)


def _attn_prompt(proj, slopes, batch, t_len, q_off, k_off, v_off):
    hpg, e = HEADS_PER_GROUP, HEAD_DIM
    in_specs = []
    for g in range(N_GROUPS):
        for off in (q_off, k_off, v_off):
            cb = off // e + g * hpg
            in_specs.append(pl.BlockSpec((t_len, e), lambda b, h, s, cb=cb: (b, cb + h)))
    grid_spec = pltpu.PrefetchScalarGridSpec(
        num_scalar_prefetch=1,
        grid=(batch, hpg),
        in_specs=in_specs,
        out_specs=pl.BlockSpec((t_len, e), lambda b, h, s: (b, h)),
        scratch_shapes=[pltpu.VMEM((N_GROUPS, t_len, e), F32),
                        pltpu.VMEM((N_GROUPS, t_len, LANES), F32)],
    )
    return pl.pallas_call(
        functools.partial(_attn_prompt_kernel, t_len=t_len),
        grid_spec=grid_spec,
        out_shape=jax.ShapeDtypeStruct((batch * t_len, hpg * e), BF16),
        compiler_params=_cparams("parallel", "parallel"),
        name="attn_prompt",
    )(slopes, *([proj] * (3 * N_GROUPS)))


def _attn_decode_kernel(q_ref, kn_ref, vn_ref, sl_ref, c0_ref, c1_ref, c2_ref, o_ref):
    caches = (c0_ref, c1_ref, c2_ref)
    outs, lses = [], []
    for g, (_, dil) in enumerate(DIL_GROUPS):
        cache = caches[g]
        q = _bf16_round(q_ref[0, g] * ATT_SCALE)
        kk = _bf16_round(cache[0, :, 0])
        vv = _bf16_round(cache[0, :, 1])
        slope = sl_ref[g][:, 0:1]
        s = jnp.sum(kk * q[None], axis=-1, keepdims=True)
        j = lax.broadcasted_iota(I32, s.shape, 0)
        s = s - slope[None] * ((SPAN - j) * dil).astype(F32)
        s_new = jnp.sum(_bf16_round(kn_ref[0, g]) * q, axis=-1, keepdims=True)
        m = jnp.maximum(jnp.max(s, axis=0), s_new)
        p = jnp.exp(s - m[None])
        p_new = jnp.exp(s_new - m)
        l = jnp.sum(p, axis=0) + p_new
        o = (jnp.sum(_bf16_round(p / l[None]) * vv, axis=0)
             + _bf16_round(p_new / l) * _bf16_round(vn_ref[0, g]))
        outs.append(o)
        lses.append(m + jnp.log(l))
    o_ref[0] = _merge_groups(lses, outs).astype(o_ref.dtype)


def _attn_decode(q, k_new, v_new, slopes_b, caches):
    n = q.shape[0]
    hpg, e = HEADS_PER_GROUP, HEAD_DIM
    qspec = pl.BlockSpec((1, N_GROUPS, hpg, e), lambda i: (i, 0, 0, 0))
    views, cspecs = [], []
    for g, (_, dil) in enumerate(DIL_GROUPS):
        views.append(caches[g].reshape(n, SPAN, dil, 2, hpg, e))
        cspecs.append(pl.BlockSpec((1, SPAN, None, 2, hpg, e), lambda i: (i, 0, 0, 0, 0, 0)))
    return pl.pallas_call(
        _attn_decode_kernel,
        grid=(n,),
        in_specs=[qspec, qspec, qspec,
                  pl.BlockSpec((N_GROUPS, hpg, LANES), lambda i: (0, 0, 0))] + cspecs,
        out_specs=pl.BlockSpec((1, hpg, e), lambda i: (i, 0, 0)),
        out_shape=jax.ShapeDtypeStruct((n, hpg, e), BF16),
        compiler_params=_cparams("parallel"),
        name="attn_decode",
    )(q, k_new, v_new, slopes_b, *views)


def _cache_shift_kernel(*refs):
    caches = refs[:N_GROUPS]
    news = refs[N_GROUPS:2 * N_GROUPS]
    outs = refs[2 * N_GROUPS:3 * N_GROUPS]
    sem = refs[3 * N_GROUPS]
    copies = []
    for g in range(N_GROUPS):
        n, w = caches[g].shape[0], caches[g].shape[1]
        step = n // CACHE_SPLIT
        for i in range(CACHE_SPLIT):
            copies.append(pltpu.make_async_copy(
                caches[g].at[pl.ds(i * step, step), pl.ds(1, w - 1)],
                outs[g].at[pl.ds(i * step, step), pl.ds(0, w - 1)],
                sem.at[len(copies)]))
        copies.append(pltpu.make_async_copy(news[g], outs[g].at[:, w - 1], sem.at[len(copies)]))
    for cp in copies:
        cp.start()
    for cp in copies:
        cp.wait()


def _cache_shift(caches, news):
    any_spec = pl.BlockSpec(memory_space=pl.ANY)
    return pl.pallas_call(
        _cache_shift_kernel,
        in_specs=[any_spec] * (2 * N_GROUPS),
        out_specs=[any_spec] * N_GROUPS,
        out_shape=[jax.ShapeDtypeStruct(c.shape, c.dtype) for c in caches],
        scratch_shapes=[pltpu.SemaphoreType.DMA((N_GROUPS * (CACHE_SPLIT + 1),))],
        name="cache_shift",
    )(*caches, *news)


def _norm_router_kernel(h_ref, g_ref, wr_ref, br_ref, hn_ref, eid_ref, gate_ref):
    h = h_ref[...]
    hn = (h * lax.rsqrt(jnp.mean(h * h, axis=-1, keepdims=True) + EPS)) * g_ref[...]
    hn_ref[...] = hn
    logits = _dot(hn.astype(BF16), wr_ref[...]) + br_ref[...]
    lane = lax.broadcasted_iota(I32, logits.shape, 1)
    ng, epg = N_EXPERT_GROUPS, EXPERTS_PER_GROUP

    def first_argmax(vals, mask):
        v = jnp.where(mask, vals, NEG)
        vmax = jnp.max(v, axis=-1, keepdims=True)
        idx = jnp.min(jnp.where(mask & (v == vmax), lane, 4 * LANES), axis=-1, keepdims=True)
        return vmax, idx

    gmask = lane < ng
    gmax, gsel = first_argmax(logits, gmask)
    p_group = 1.0 / jnp.sum(jnp.where(gmask, jnp.exp(logits - gmax), 0.0), axis=-1, keepdims=True)
    lo = ng + gsel * epg
    emask = (lane >= lo) & (lane < lo + epg)
    v1, i1 = first_argmax(logits, emask)
    v2, i2 = first_argmax(logits, emask & (lane != i1))
    e2 = jnp.exp(v2 - v1)
    g1 = p_group / (1.0 + e2)
    g2 = p_group * e2 / (1.0 + e2)
    eid_ref[...] = jnp.where(lane == 0, i1 - ng, jnp.where(lane == 1, i2 - ng, 0))
    gate_ref[...] = jnp.where(lane == 0, g1, jnp.where(lane == 1, g2, 0.0))


def _norm_router(h, g, w_router, b_router):
    m, d = h.shape
    tm = _row_tile(m, 256)
    row = lambda i: (i, 0)
    fixed = lambda i: (0, 0)
    return pl.pallas_call(
        _norm_router_kernel,
        grid=(m // tm,),
        in_specs=[pl.BlockSpec((tm, d), row),
                  pl.BlockSpec((1, d), fixed),
                  pl.BlockSpec((d, LANES), fixed),
                  pl.BlockSpec((1, LANES), fixed)],
        out_specs=[pl.BlockSpec((tm, d), row),
                   pl.BlockSpec((tm, LANES), row),
                   pl.BlockSpec((tm, LANES), row)],
        out_shape=[jax.ShapeDtypeStruct((m, d), F32),
                   jax.ShapeDtypeStruct((m, LANES), I32),
                   jax.ShapeDtypeStruct((m, LANES), F32)],
        compiler_params=_cparams("parallel"),
        name="norm_router",
    )(h, g.reshape(1, d), w_router, b_router)


def _for_row_groups(rows, fn):
    big = 2 * MOE_GRAN

    def pair(i, _):
        fn(pl.multiple_of(i * big, big), big)
        return 0

    n_pair = rows // big
    lax.fori_loop(0, n_pair, pair, 0)

    @pl.when(rows % big != 0)
    def _():
        fn(pl.multiple_of(n_pair * big, big), MOE_GRAN)


def _zero_tail(ref, rows):
    def fill(i, _):
        ref[pl.ds(pl.multiple_of(i * MOE_GRAN, MOE_GRAN), MOE_GRAN), :] = jnp.zeros(
            (MOE_GRAN, ref.shape[1]), ref.dtype)
        return 0

    lax.fori_loop(rows // MOE_GRAN, ref.shape[0] // MOE_GRAN, fill, 0)


def _moe_up_kernel(sb_exp, sb_rows, row_tok, x_hbm, wg_ref, wu_ref, hb_ref,
                   stage_ref, xb_ref, wgb_ref, wub_ref, sem, *, rmax):
    s = pl.program_id(0)
    f = pl.program_id(1)
    rows = sb_rows[s]
    gran = MOE_GRAN

    @pl.when((rows > 0) & (f == 0))
    def _gather():
        def group(gi, _):
            r0 = pl.multiple_of(gi * gran, gran)

            def issue(i, _):
                tok = row_tok[s * rmax + r0 + i]
                pltpu.make_async_copy(x_hbm.at[pl.ds(tok, 1), :],
                                      stage_ref.at[pl.ds(i, 1), :], sem).start()
                return 0

            lax.fori_loop(0, gran, issue, 0)

            def drain(i, _):
                pltpu.make_async_copy(x_hbm.at[pl.ds(0, 1), :],
                                      stage_ref.at[pl.ds(i, 1), :], sem).wait()
                return 0

            lax.fori_loop(0, gran, drain, 0)
            xb_ref[pl.ds(r0, gran), :] = stage_ref[...].astype(BF16)
            return 0

        lax.fori_loop(0, rows // gran, group, 0)

    @pl.when(rows > 0)
    def _compute():
        wgb_ref[...] = wg_ref[0].astype(BF16)
        wub_ref[...] = wu_ref[0].astype(BF16)

        def sub(r0, size):
            x = xb_ref[pl.ds(r0, size), :]
            gate = _dot(x, wgb_ref[...])
            up = _dot(x, wub_ref[...])
            hb_ref[pl.ds(r0, size), :] = (gate * _sigmoid(gate) * up).astype(hb_ref.dtype)

        _for_row_groups(rows, sub)

    _zero_tail(hb_ref, rows)


def _moe_down_kernel(sb_exp, sb_rows, hb_ref, wd_ref, y_ref):
    rows = sb_rows[pl.program_id(0)]

    @pl.when(rows > 0)
    def _compute():
        wd = wd_ref[0].astype(BF16)

        def sub(r0, size):
            y_ref[pl.ds(r0, size), :] = _dot(hb_ref[pl.ds(r0, size), :], wd)

        _for_row_groups(rows, sub)

    _zero_tail(y_ref, rows)


def _moe_ffn(hn, sb_exp, sb_rows, row_tok, w_gate, w_up, w_down):
    d = hn.shape[1]
    fdim = w_gate.shape[2]
    rmax = MOE_ROWS
    n_sb = sb_exp.shape[0]
    fc = min(fdim, MOE_HIDDEN_CHUNK)
    nf = fdim // fc

    def chunk(r, s, j, last):
        return jnp.where(r[s] > 0, j, last)

    up_spec = pltpu.PrefetchScalarGridSpec(
        num_scalar_prefetch=3,
        grid=(n_sb, nf),
        in_specs=[pl.BlockSpec(memory_space=pl.ANY),
                  pl.BlockSpec((1, d, fc), lambda s, f, e, r, t: (e[s], 0, chunk(r, s, f, nf - 1))),
                  pl.BlockSpec((1, d, fc), lambda s, f, e, r, t: (e[s], 0, chunk(r, s, f, nf - 1)))],
        out_specs=pl.BlockSpec((rmax, fc), lambda s, f, e, r, t: (s, f)),
        scratch_shapes=[pltpu.VMEM((MOE_GRAN, d), F32),
                        pltpu.VMEM((rmax, d), BF16),
                        pltpu.VMEM((d, fc), BF16),
                        pltpu.VMEM((d, fc), BF16),
                        pltpu.SemaphoreType.DMA(())],
    )
    hb = pl.pallas_call(
        functools.partial(_moe_up_kernel, rmax=rmax),
        grid_spec=up_spec,
        out_shape=jax.ShapeDtypeStruct((n_sb * rmax, fdim), BF16),
        compiler_params=_cparams("arbitrary", "arbitrary"),
        name="moe_up",
    )(sb_exp, sb_rows, row_tok, hn, w_gate, w_up)
    tn = _col_tile(d)
    nn = d // tn
    down_spec = pltpu.PrefetchScalarGridSpec(
        num_scalar_prefetch=2,
        grid=(n_sb, nn),
        in_specs=[pl.BlockSpec((rmax, fdim), lambda s, j, e, r: (s, 0)),
                  pl.BlockSpec((1, fdim, tn), lambda s, j, e, r: (e[s], 0, chunk(r, s, j, nn - 1)))],
        out_specs=pl.BlockSpec((rmax, tn), lambda s, j, e, r: (s, j)),
    )
    return pl.pallas_call(
        _moe_down_kernel,
        grid_spec=down_spec,
        out_shape=jax.ShapeDtypeStruct((n_sb * rmax, d), F32),
        compiler_params=_cparams("arbitrary", "arbitrary"),
        name="moe_down",
    )(sb_exp, sb_rows, hb, w_down)


def _combine_kernel(pos_ref, h_ref, gate_ref, g_ref, yb_hbm, yp_ref, ys_ref, buf_ref, sem, *, nb_p):
    i = pl.program_id(0)
    tb = h_ref.shape[0]

    def issue(t, _):
        for k in range(TOP_K):
            row = pos_ref[(i * tb + t) * TOP_K + k]
            pltpu.make_async_copy(yb_hbm.at[pl.ds(row, 1), :],
                                  buf_ref.at[k, pl.ds(t, 1), :], sem).start()
        return 0

    lax.fori_loop(0, tb, issue, 0)

    def drain(t, _):
        for k in range(TOP_K):
            pltpu.make_async_copy(yb_hbm.at[pl.ds(0, 1), :],
                                  buf_ref.at[k, pl.ds(t, 1), :], sem).wait()
        return 0

    lax.fori_loop(0, tb, drain, 0)
    gate = gate_ref[...]
    y = h_ref[...] + gate[:, 0:1] * buf_ref[0] + gate[:, 1:2] * buf_ref[1]
    y = (y * lax.rsqrt(jnp.mean(y * y, axis=-1, keepdims=True) + EPS)) * g_ref[...]

    @pl.when(i < nb_p)
    def _():
        yp_ref[...] = y

    @pl.when(i == nb_p)
    def _():
        ys_ref[...] = y


def _combine(pos, h, gate, g_final, yb, m_p):
    m, d = h.shape
    tb = m - m_p
    assert m_p % tb == 0 and tb % SUBLANES == 0
    nb_p = m_p // tb
    grid_spec = pltpu.PrefetchScalarGridSpec(
        num_scalar_prefetch=1,
        grid=(nb_p + 1,),
        in_specs=[pl.BlockSpec((tb, d), lambda i, p: (i, 0)),
                  pl.BlockSpec((tb, LANES), lambda i, p: (i, 0)),
                  pl.BlockSpec((1, d), lambda i, p: (0, 0)),
                  pl.BlockSpec(memory_space=pl.ANY)],
        out_specs=[pl.BlockSpec((tb, d), lambda i, p: (jnp.minimum(i, nb_p - 1), 0)),
                   pl.BlockSpec((tb, d), lambda i, p: (0, 0))],
        scratch_shapes=[pltpu.VMEM((TOP_K, tb, d), F32), pltpu.SemaphoreType.DMA(())],
    )
    return pl.pallas_call(
        functools.partial(_combine_kernel, nb_p=nb_p),
        grid_spec=grid_spec,
        out_shape=[jax.ShapeDtypeStruct((m_p, d), F32), jax.ShapeDtypeStruct((tb, d), F32)],
        compiler_params=_cparams("arbitrary"),
        name="combine",
    )(pos, h, gate, g_final.reshape(1, d), yb)


def _dispatch_plan(eid, n_tok):
    a = n_tok * TOP_K
    rmax, gran = MOE_ROWS, MOE_GRAN
    n_sb = -(-a // rmax) + N_EXPERTS
    e_flat = eid.reshape(a)
    onehot = (e_flat[:, None] == jnp.arange(N_EXPERTS, dtype=I32)[None, :]).astype(I32)
    csum = jnp.cumsum(onehot, axis=0)
    counts = csum[-1]
    rank = jnp.take_along_axis(csum, e_flat[:, None], axis=1)[:, 0] - 1
    padded = (counts + gran - 1) // gran * gran
    slabs = (padded + rmax - 1) // rmax
    slab_end = jnp.cumsum(slabs)
    slab_start = slab_end - slabs
    total = slab_end[-1]
    pos = slab_start[e_flat] * rmax + rank
    sidx = jnp.arange(n_sb, dtype=I32)
    s_eff = jnp.minimum(sidx, jnp.maximum(total - 1, 0))
    exp_of = jnp.minimum(jnp.searchsorted(slab_end, s_eff, side="right"), N_EXPERTS - 1).astype(I32)
    local = s_eff - slab_start[exp_of]
    rows = jnp.clip(padded[exp_of] - local * rmax, 0, rmax)
    rows = jnp.where(sidx < total, rows, 0).astype(I32)
    tok = jnp.repeat(jnp.arange(n_tok, dtype=I32), TOP_K)
    row_tok = jnp.zeros((n_sb * rmax,), I32).at[pos].set(tok)
    return exp_of, rows, row_tok, pos.astype(I32)


def kernel(x_prompt, x_sample, cache_kv0, cache_kv1, cache_kv2, state_conv, state_rglru, norm1, w_in,
           conv_w, conv_b, lru_wa, lru_ba, lru_wx, lru_bx, lru_lambda, w_branch_a, w_branch_b, w_out,
           norm2, router_group_w, router_group_b, router_expert_w, router_expert_b, expert_w_gate,
           expert_w_up, expert_w_down, norm_final):
    batch, seq, d = x_prompt.shape
    n_dec, dec_seq, _ = x_sample.shape
    assert w_in.shape[0] == 1 and dec_seq == 1
    hpg, e = HEADS_PER_GROUP, HEAD_DIM
    d_att = N_GROUPS * hpg * e
    q_off, k_off, v_off = 2 * d, 2 * d + d_att, 2 * d + 2 * d_att
    ga_off, gb_off = 2 * d + 3 * d_att, 3 * d + 3 * d_att
    caches = (cache_kv0[0], cache_kv1[0], cache_kv2[0])
    for c, (win, dil) in zip(caches, DIL_GROUPS):
        assert c.shape[1] == win == SPAN * dil and seq % (SPAN * dil) == 0
    m_p = batch * seq
    n_tok = m_p + n_dec
    slopes = 2.0 ** (-8.0 * jnp.arange(1, N_GROUPS * hpg + 1, dtype=F32) / (N_GROUPS * hpg))
    n_pad = LANES - N_EXPERT_GROUPS - N_EXPERTS
    w_router = jnp.concatenate(
        [router_group_w[0], router_expert_w[0], jnp.zeros((d, n_pad), F32)], axis=1).astype(BF16)
    b_router = jnp.concatenate(
        [router_group_b[0], router_expert_b[0], jnp.zeros((n_pad,), F32)]).reshape(1, LANES)
    rnn_w = (conv_w[0], conv_b[0], lru_wa[0], lru_ba[0], lru_wx[0], lru_bx[0], lru_lambda[0])

    x_all = jnp.concatenate([x_prompt.reshape(m_p, d), x_sample.reshape(n_dec, d)], axis=0)
    xn = _rmsnorm(x_all, norm1[0], BF16)
    proj = _matmul(xn, w_in[0], F32)
    proj_s = proj[m_p:]

    a_p, ph = _rglru_prompt(proj, batch, seq, d, *rnn_w)
    o_p = _attn_prompt(proj, slopes, batch, seq, q_off, k_off, v_off)

    conv_t = jnp.transpose(state_conv[0], (1, 0, 2))
    a_s, new_conv_t, new_h = _rglru_step(proj, m_p, n_dec, conv_t, state_rglru[0], d, *rnn_w)
    heads = lambda off: proj_s[:, off:off + d_att].reshape(n_dec, N_GROUPS, hpg, e)
    q_s, k_s, v_s = heads(q_off), heads(k_off), heads(v_off)
    slopes_b = jnp.broadcast_to(slopes.reshape(N_GROUPS, hpg, 1), (N_GROUPS, hpg, LANES))
    o_s = _attn_decode(q_s, k_s, v_s, slopes_b, caches).reshape(n_dec, hpg * e)
    kv_new = [jnp.stack([k_s[:, g], v_s[:, g]], axis=1) for g in range(N_GROUPS)]
    s_kv = _cache_shift(caches, kv_new)

    a_all = jnp.concatenate([a_p, a_s], axis=0)
    o_all = jnp.concatenate([o_p, o_s], axis=0)
    u = _merge(a_all, o_all, w_branch_a[0], w_branch_b[0], proj, ga_off, gb_off)
    h = _outproj(x_all, u, w_out[0])

    hn, eid, gate = _norm_router(h, norm2[0], w_router, b_router)
    sb_exp, sb_rows, row_tok, pos = _dispatch_plan(eid[:, :TOP_K], n_tok)
    yb = _moe_ffn(hn, sb_exp, sb_rows, row_tok, expert_w_gate[0], expert_w_up[0], expert_w_down[0])
    y_p, y_s = _combine(pos, h, gate, norm_final, yb, m_p)

    def heads_p(off, g):
        cols = proj[:m_p, off + g * hpg * e:off + (g + 1) * hpg * e]
        return cols.reshape(batch, seq, hpg, e)

    p_kv = []
    for g, (win, _) in enumerate(DIL_GROUPS):
        w = min(win, seq)
        p_kv.append(jnp.stack([heads_p(k_off, g)[:, -w:], heads_p(v_off, g)[:, -w:]], axis=2)[None])
    p_conv = proj[:m_p, :d].reshape(batch, seq, d)[:, -(CONV_W - 1):][None]
    p_h = ph.reshape(batch, d)[None]
    s_conv = jnp.transpose(new_conv_t, (1, 0, 2))[None]
    return (y_p.reshape(batch, seq, d), y_s.reshape(n_dec, 1, d), p_kv[0], p_kv[1], p_kv[2],
            p_conv, p_h, s_kv[0][None], s_kv[1][None], s_kv[2][None], s_conv, new_h[None])
```

```python
import functools

import jax
import jax.numpy as jnp
from jax import lax
from jax.experimental import pallas as pl
from jax.experimental.pallas import tpu as pltpu

F32, BF16, I32 = jnp.float32, jnp.bfloat16, jnp.int32

N_RNN_BLOCKS = 16
CONV_W = 4
LRU_C = 8.0
HEAD_DIM = 128
HEADS_PER_GROUP = 8
DIL_GROUPS = ((128, 1), (512, 4), (2048, 16))
N_GROUPS = len(DIL_GROUPS)
SPAN = 128
N_EXPERT_GROUPS = 4
EXPERTS_PER_GROUP = 8
N_EXPERTS = N_EXPERT_GROUPS * EXPERTS_PER_GROUP
TOP_K = 2
EPS = 1e-6
PAST_LEN = 2048
ATT_SCALE = HEAD_DIM ** -0.5
NEG = -1e30

LANES = 128
SUBLANES = 8
VMEM_LIMIT = 56 * 1024 * 1024
ROW_TILE = 1024
COL_TILE = 512
MOE_ROWS = 1024
MOE_GRAN = 128
MOE_HIDDEN_CHUNK = 256
SCAN_CHUNKS = 16
SCAN_PITCH_PAD = 8
CACHE_ROWS = 512
ATTN_UNROLL = 4


def _cparams(*sem):
    return pltpu.CompilerParams(dimension_semantics=sem, vmem_limit_bytes=VMEM_LIMIT)


def _bf16_round(x):
    return x.astype(BF16).astype(F32)


def _sigmoid(x):
    return 1.0 / (1.0 + jnp.exp(-x))


def _gelu_tanh(x):
    return 0.5 * x * (1.0 + jnp.tanh(0.7978845608028654 * (x + 0.044715 * (x * x * x))))


def _one_minus_exp2x(x):
    t = jnp.tanh(x)
    return (-2.0 * t) / (1.0 - t)


def _softplus(x):
    return jnp.maximum(x, 0.0) + jnp.log1p(jnp.exp(-jnp.abs(x)))


def _dot(a, b):
    return jnp.dot(a, b, preferred_element_type=F32)


def _col_tile(n, *offsets):
    t = COL_TILE
    while t > LANES and (n % t or any(o % t for o in offsets)):
        t //= 2
    return t


def _row_tile(m, pref=ROW_TILE):
    best = None
    for t in range(LANES, min(m, pref) + 1, LANES):
        if m % t == 0:
            best = t
    return best or m


def _rmsnorm_kernel(x_ref, g_ref, o_ref):
    x = x_ref[...]
    y = x * lax.rsqrt(jnp.mean(x * x, axis=-1, keepdims=True) + EPS)
    o_ref[...] = (y * g_ref[...]).astype(o_ref.dtype)


def _rmsnorm(x, g, out_dtype):
    m, d = x.shape
    tm = _row_tile(m, 256)
    return pl.pallas_call(
        _rmsnorm_kernel,
        grid=(m // tm,),
        in_specs=[pl.BlockSpec((tm, d), lambda i: (i, 0)),
                  pl.BlockSpec((1, d), lambda i: (0, 0))],
        out_specs=pl.BlockSpec((tm, d), lambda i: (i, 0)),
        out_shape=jax.ShapeDtypeStruct((m, d), out_dtype),
        compiler_params=_cparams("parallel"),
        name="rmsnorm",
    )(x, g.reshape(1, d))


def _mm_kernel(x_ref, w_ref, o_ref, wb_ref):
    @pl.when(pl.program_id(1) == 0)
    def _():
        wb_ref[...] = w_ref[...].astype(BF16)

    o_ref[...] = _dot(x_ref[...], wb_ref[...]).astype(o_ref.dtype)


def _matmul(x, w, out_dtype):
    m, k = x.shape
    n = w.shape[1]
    tm = _row_tile(m)
    tn = 2 * COL_TILE if n % (2 * COL_TILE) == 0 else _col_tile(n)
    return pl.pallas_call(
        _mm_kernel,
        grid=(n // tn, m // tm),
        in_specs=[pl.BlockSpec((tm, k), lambda j, i: (i, 0)),
                  pl.BlockSpec((k, tn), lambda j, i: (0, j), pipeline_mode=pl.Buffered(1))],
        out_specs=pl.BlockSpec((tm, tn), lambda j, i: (i, j)),
        out_shape=jax.ShapeDtypeStruct((m, n), out_dtype),
        scratch_shapes=[pltpu.VMEM((k, tn), BF16)],
        compiler_params=_cparams("parallel", "arbitrary"),
        name="matmul",
    )(x, w)


def _merge_kernel(a_ref, o_ref, wa_ref, wb_ref, ga_ref, gb_ref, u_ref, wab_ref, wbb_ref):
    @pl.when(pl.program_id(1) == 0)
    def _():
        wab_ref[...] = wa_ref[...].astype(BF16)
        wbb_ref[...] = wb_ref[...].astype(BF16)

    ya = _dot(a_ref[...], wab_ref[...])
    yb = _dot(o_ref[...], wbb_ref[...])
    u = _sigmoid(ga_ref[...]) * ya + _sigmoid(gb_ref[...]) * yb
    u_ref[...] = u.astype(u_ref.dtype)


def _merge(a, o, w_a, w_b, proj, ga_off, gb_off):
    m, ka = a.shape
    kb = o.shape[1]
    n = w_a.shape[1]
    tm = _row_tile(m)
    tn = _col_tile(n, ga_off, gb_off)
    return pl.pallas_call(
        _merge_kernel,
        grid=(n // tn, m // tm),
        in_specs=[pl.BlockSpec((tm, ka), lambda j, i: (i, 0)),
                  pl.BlockSpec((tm, kb), lambda j, i: (i, 0)),
                  pl.BlockSpec((ka, tn), lambda j, i: (0, j)),
                  pl.BlockSpec((kb, tn), lambda j, i: (0, j)),
                  pl.BlockSpec((tm, tn), lambda j, i: (i, j + ga_off // tn)),
                  pl.BlockSpec((tm, tn), lambda j, i: (i, j + gb_off // tn))],
        out_specs=pl.BlockSpec((tm, tn), lambda j, i: (i, j)),
        out_shape=jax.ShapeDtypeStruct((m, n), BF16),
        scratch_shapes=[pltpu.VMEM((ka, tn), BF16), pltpu.VMEM((kb, tn), BF16)],
        compiler_params=_cparams("parallel", "arbitrary"),
        name="merge",
    )(a, o, w_a, w_b, proj, proj)


def _outproj_kernel(x_ref, u_ref, w_ref, h_ref, wb_ref):
    @pl.when(pl.program_id(1) == 0)
    def _():
        wb_ref[...] = w_ref[...].astype(BF16)

    h_ref[...] = x_ref[...] + _dot(u_ref[...], wb_ref[...])


def _outproj(x, u, w):
    m, k = u.shape
    n = w.shape[1]
    tm = _row_tile(m)
    tn = _col_tile(n)
    return pl.pallas_call(
        _outproj_kernel,
        grid=(n // tn, m // tm),
        in_specs=[pl.BlockSpec((tm, tn), lambda j, i: (i, j)),
                  pl.BlockSpec((tm, k), lambda j, i: (i, 0)),
                  pl.BlockSpec((k, tn), lambda j, i: (0, j))],
        out_specs=pl.BlockSpec((tm, tn), lambda j, i: (i, j)),
        out_shape=jax.ShapeDtypeStruct((m, n), F32),
        scratch_shapes=[pltpu.VMEM((k, tn), BF16)],
        compiler_params=_cparams("parallel", "arbitrary"),
        name="outproj",
    )(x, u, w)


def _rglru_prompt_kernel(xr_ref, yr_ref, cw_ref, cb_ref, wa_ref, ba_ref, wx_ref, bx_ref, lam_ref,
                         a_out_ref, ph_ref,
                         xe_ref, a_ref, u_ref, hs_ref, as_ref, hsz_ref, *, t_len, nc):
    c = xr_ref.shape[1]
    nl = c // LANES
    cl = t_len // nc
    pitch = cl + SCAN_PITCH_PAD
    xe_ref[0:SUBLANES, :] = jnp.zeros((SUBLANES, c), F32)
    xe_ref[SUBLANES:SUBLANES + t_len, :] = _bf16_round(xr_ref[...])
    wa = wa_ref[0].astype(BF16)
    wx = wx_ref[0].astype(BF16)
    decay = (-LRU_C) * _softplus(-lam_ref[...])
    cw = _bf16_round(cw_ref[...])
    for ch in range(nc):
        r0 = ch * cl
        xc = cw[0:1] * xe_ref[r0 + 5:r0 + 5 + cl, :]
        for j in range(1, CONV_W):
            xc = xc + cw[j:j + 1] * xe_ref[r0 + 5 + j:r0 + 5 + j + cl, :]
        xc = xc + cb_ref[...]
        xcb = xc.astype(BF16)
        r = _sigmoid(_dot(xcb, wa) + ba_ref[...])
        i = _sigmoid(_dot(xcb, wx) + bx_ref[...])
        log_a = r * decay
        a = jnp.exp(log_a)
        mult = jnp.sqrt(_one_minus_exp2x(log_a))
        if ch == 0:
            row = lax.broadcasted_iota(I32, (cl, c), 0)
            a = jnp.where(row == 0, 0.0, a)
            mult = jnp.where(row == 0, 1.0, mult)
        au = (a, mult * (i * xc))
        for lb in range(nl):
            for ref, val in zip((a_ref, u_ref), au):
                ref[lb, ch * pitch:ch * pitch + cl, :] = val[:, lb * LANES:(lb + 1) * LANES]

    def pass1(j, carry):
        out = []
        for lb in range(nl):
            acc_a, acc_h = carry[2 * lb], carry[2 * lb + 1]
            a = a_ref[lb, pl.ds(j, nc, stride=pitch), :]
            u = u_ref[lb, pl.ds(j, nc, stride=pitch), :]
            acc_a = a * acc_a
            acc_h = a * acc_h + u
            as_ref[lb, j] = acc_a
            hsz_ref[lb, j] = acc_h
            out += [acc_a, acc_h]
        return tuple(out)

    init = (jnp.ones((nc, LANES), F32), jnp.zeros((nc, LANES), F32)) * nl
    fin = lax.fori_loop(0, cl, pass1, init)
    cins = []
    for lb in range(nl):
        fin_a, fin_h = fin[2 * lb], fin[2 * lb + 1]
        carry = jnp.zeros((1, LANES), F32)
        rows = []
        for ch in range(nc):
            rows.append(carry)
            carry = fin_a[ch:ch + 1] * carry + fin_h[ch:ch + 1]
        cins.append(jnp.concatenate(rows, axis=0))
        ph_ref[0, :, lb * LANES:(lb + 1) * LANES] = carry

    def pass2(j, _):
        for lb in range(nl):
            hs_ref[lb, pl.ds(j, nc, stride=pitch), :] = hsz_ref[lb, j] + as_ref[lb, j] * cins[lb]
        return 0

    lax.fori_loop(0, cl, pass2, 0)
    for ch in range(nc):
        for lb in range(nl):
            y = yr_ref[ch * cl:(ch + 1) * cl, lb * LANES:(lb + 1) * LANES]
            a_out_ref[ch * cl:(ch + 1) * cl, lb * LANES:(lb + 1) * LANES] = (
                _gelu_tanh(y) * hs_ref[lb, ch * pitch:ch * pitch + cl, :]).astype(a_out_ref.dtype)


def _rglru_prompt(proj, batch, t_len, d, conv_w, conv_b, lru_wa, lru_ba, lru_wx, lru_bx, lam):
    nb = lru_wa.shape[0]
    c = d // nb
    nl = c // LANES
    nc = SCAN_CHUNKS
    cl = t_len // nc
    pitch = cl + SCAN_PITCH_PAD
    vec = lambda v: v.reshape(1, d)
    vspec = pl.BlockSpec((1, c), lambda b, n: (0, n))
    wspec = pl.BlockSpec((1, c, c), lambda b, n: (n, 0, 0))
    return pl.pallas_call(
        functools.partial(_rglru_prompt_kernel, t_len=t_len, nc=nc),
        grid=(batch, nb),
        in_specs=[pl.BlockSpec((t_len, c), lambda b, n: (b, n)),
                  pl.BlockSpec((t_len, c), lambda b, n: (b, n + nb)),
                  pl.BlockSpec((CONV_W, c), lambda b, n: (0, n)),
                  vspec, wspec, vspec, wspec, vspec, vspec],
        out_specs=[pl.BlockSpec((t_len, c), lambda b, n: (b, n)),
                   pl.BlockSpec((1, 1, c), lambda b, n: (b, 0, n))],
        out_shape=[jax.ShapeDtypeStruct((batch * t_len, d), BF16),
                   jax.ShapeDtypeStruct((batch, 1, d), F32)],
        scratch_shapes=[pltpu.VMEM((t_len + SUBLANES, c), F32),
                        pltpu.VMEM((nl, nc * pitch, LANES), F32),
                        pltpu.VMEM((nl, nc * pitch, LANES), F32),
                        pltpu.VMEM((nl, nc * pitch, LANES), F32),
                        pltpu.VMEM((nl, cl, nc, LANES), F32),
                        pltpu.VMEM((nl, cl, nc, LANES), F32)],
        compiler_params=_cparams("parallel", "parallel"),
        name="rglru_prompt",
    )(proj, proj, conv_w, vec(conv_b), lru_wa, vec(lru_ba), lru_wx, vec(lru_bx), vec(lam))


def _rglru_step_kernel(xr_ref, yr_ref, cs_ref, h0_ref, cw_ref, cb_ref, wa_ref, ba_ref, wx_ref, bx_ref,
                       lam_ref, a_out_ref, ncs_ref, nh_ref, *, reset):
    x = xr_ref[...]
    cw = _bf16_round(cw_ref[...])
    xc = cw[CONV_W - 1:CONV_W] * _bf16_round(x)
    for j in range(CONV_W - 1):
        xc = xc + cw[j:j + 1] * _bf16_round(cs_ref[j])
    xc = xc + cb_ref[...]
    xcb = xc.astype(BF16)
    r = _sigmoid(_dot(xcb, wa_ref[0].astype(BF16)) + ba_ref[...])
    i = _sigmoid(_dot(xcb, wx_ref[0].astype(BF16)) + bx_ref[...])
    log_a = r * ((-LRU_C) * _softplus(-lam_ref[...]))
    if reset:
        h = i * xc
    else:
        h = jnp.exp(log_a) * h0_ref[...] + jnp.sqrt(_one_minus_exp2x(log_a)) * (i * xc)
    nh_ref[...] = h
    a_out_ref[...] = (_gelu_tanh(yr_ref[...]) * h).astype(a_out_ref.dtype)
    for j in range(CONV_W - 2):
        ncs_ref[j] = cs_ref[j + 1]
    ncs_ref[CONV_W - 2] = x


def _rglru_step(proj, row_off, n, conv_state, h0, d, conv_w, conv_b, lru_wa, lru_ba, lru_wx, lru_bx,
                lam):
    nb = lru_wa.shape[0]
    c = d // nb
    assert row_off % n == 0
    rb = row_off // n
    vec = lambda v: v.reshape(1, d)
    vspec = pl.BlockSpec((1, c), lambda j: (0, j))
    wspec = pl.BlockSpec((1, c, c), lambda j: (j, 0, 0))
    xspec = pl.BlockSpec((n, c), lambda j: (0, j))
    sspec = pl.BlockSpec((CONV_W - 1, n, c), lambda j: (0, 0, j))
    return pl.pallas_call(
        functools.partial(_rglru_step_kernel, reset=(PAST_LEN == 0)),
        grid=(nb,),
        in_specs=[pl.BlockSpec((n, c), lambda j: (rb, j)),
                  pl.BlockSpec((n, c), lambda j: (rb, j + nb)), sspec, xspec,
                  pl.BlockSpec((CONV_W, c), lambda j: (0, j)),
                  vspec, wspec, vspec, wspec, vspec, vspec],
        out_specs=[xspec, sspec, xspec],
        out_shape=[jax.ShapeDtypeStruct((n, d), BF16),
                   jax.ShapeDtypeStruct((CONV_W - 1, n, d), F32),
                   jax.ShapeDtypeStruct((n, d), F32)],
        compiler_params=_cparams("parallel"),
        name="rglru_step",
    )(proj, proj, conv_state, h0, conv_w, vec(conv_b), lru_wa, vec(lru_ba), lru_wx, vec(lru_bx),
      vec(lam))


def _unroll_for(trips):
    return max(u for u in range(1, ATTN_UNROLL + 1) if trips % u == 0)


def _merge_groups(lses, outs):
    m = functools.reduce(jnp.maximum, lses)
    es = [jnp.exp(x - m) for x in lses]
    den = functools.reduce(lambda a, b: a + b, es)
    acc = _bf16_round(es[0] / den) * _bf16_round(outs[0])
    for g in range(1, len(outs)):
        acc = acc + _bf16_round(es[g] / den) * _bf16_round(outs[g])
    return acc


def _attn_prompt_kernel(slope_ref, *refs, t_len):
    qkv = refs[:3 * N_GROUPS]
    o_ref = refs[3 * N_GROUPS]
    og_ref, lse_ref = refs[3 * N_GROUPS + 1:]
    h = pl.program_id(1)

    for g, (_, dil) in enumerate(DIL_GROUPS):
        q_ref, k_ref, v_ref = qkv[3 * g:3 * g + 3]
        slope = slope_ref[g * HEADS_PER_GROUP + h] * float(dil)
        n_blk = t_len // dil // SPAN

        def rows(ref, start, size, dil=dil):
            if dil == 1:
                return ref[pl.ds(start, size), :]
            return ref[pl.ds(start, size, stride=dil), :]

        def put(ref, g, start, val, dil=dil):
            if dil == 1:
                ref[g, pl.ds(start, SPAN), :] = val
            else:
                ref[g, pl.ds(start, SPAN, stride=dil), :] = val

        def block(base, with_prev, g=g, dil=dil, slope=slope, q_ref=q_ref, k_ref=k_ref,
                  v_ref=v_ref, rows=rows, put=put):
            nk = 2 * SPAN if with_prev else SPAN
            k0 = base - SPAN * dil if with_prev else base
            q = (rows(q_ref, base, SPAN) * ATT_SCALE).astype(BF16)
            kk = rows(k_ref, k0, nk).astype(BF16)
            vv = rows(v_ref, k0, nk).astype(BF16)
            s = lax.dot_general(q, kk, (((1,), (1,)), ((), ())), preferred_element_type=F32)
            qi = lax.broadcasted_iota(I32, (SPAN, nk), 0)
            kj = lax.broadcasted_iota(I32, (SPAN, nk), 1)
            diff = qi - kj + (SPAN if with_prev else 0)
            valid = (diff >= 0) & (diff <= SPAN)
            s = jnp.where(valid, s - slope * diff.astype(F32), NEG)
            m = jnp.max(s, axis=-1, keepdims=True)
            p = jnp.exp(s - m)
            l = jnp.sum(p, axis=-1, keepdims=True)
            put(og_ref, g, base, _dot((p / l).astype(BF16), vv))
            put(lse_ref, g, base, jnp.broadcast_to(m + jnp.log(l), (SPAN, LANES)))

        def per_class(r, _, block=block, n_blk=n_blk, dil=dil):
            block(r, False)

            def later(n, _):
                block(n * (SPAN * dil) + r, True)
                return 0

            if n_blk > 1:
                lax.fori_loop(1, n_blk, later, 0, unroll=_unroll_for(n_blk - 1))
            return 0

        if dil == 1:
            per_class(0, 0)
        else:
            lax.fori_loop(0, dil, per_class, 0, unroll=_unroll_for(dil) if n_blk == 1 else 1)

    rc = 256
    for c0 in range(0, t_len, rc):
        acc = _merge_groups([lse_ref[g, c0:c0 + rc, :] for g in range(N_GROUPS)],
                            [og_ref[g, c0:c0 + rc, :] for g in range(N_GROUPS)])
        o_ref[c0:c0 + rc, :] = acc.astype(o_ref.dtype)


def _attn_prompt(proj, slopes, batch, t_len, q_off, k_off, v_off):
    hpg, e = HEADS_PER_GROUP, HEAD_DIM
    in_specs = []
    for g in range(N_GROUPS):
        for off in (q_off, k_off, v_off):
            cb = off // e + g * hpg
            in_specs.append(pl.BlockSpec((t_len, e), lambda b, h, s, cb=cb: (b, cb + h)))
    grid_spec = pltpu.PrefetchScalarGridSpec(
        num_scalar_prefetch=1,
        grid=(batch, hpg),
        in_specs=in_specs,
        out_specs=pl.BlockSpec((t_len, e), lambda b, h, s: (b, h)),
        scratch_shapes=[pltpu.VMEM((N_GROUPS, t_len, e), F32),
                        pltpu.VMEM((N_GROUPS, t_len, LANES), F32)],
    )
    return pl.pallas_call(
        functools.partial(_attn_prompt_kernel, t_len=t_len),
        grid_spec=grid_spec,
        out_shape=jax.ShapeDtypeStruct((batch * t_len, hpg * e), BF16),
        compiler_params=_cparams("parallel", "parallel"),
        name="attn_prompt",
    )(slopes, *([proj] * (3 * N_GROUPS)))


def _attn_decode_kernel(q_ref, kn_ref, vn_ref, sl_ref, c0_ref, c1_ref, c2_ref, o_ref):
    caches = (c0_ref, c1_ref, c2_ref)
    outs, lses = [], []
    for g, (_, dil) in enumerate(DIL_GROUPS):
        cache = caches[g]
        q = _bf16_round(q_ref[0, g] * ATT_SCALE)
        kk = _bf16_round(cache[0, :, 0])
        vv = _bf16_round(cache[0, :, 1])
        slope = sl_ref[g][:, 0:1]
        s = jnp.sum(kk * q[None], axis=-1, keepdims=True)
        j = lax.broadcasted_iota(I32, s.shape, 0)
        s = s - slope[None] * ((SPAN - j) * dil).astype(F32)
        s_new = jnp.sum(_bf16_round(kn_ref[0, g]) * q, axis=-1, keepdims=True)
        m = jnp.maximum(jnp.max(s, axis=0), s_new)
        p = jnp.exp(s - m[None])
        p_new = jnp.exp(s_new - m)
        l = jnp.sum(p, axis=0) + p_new
        o = (jnp.sum(_bf16_round(p / l[None]) * vv, axis=0)
             + _bf16_round(p_new / l) * _bf16_round(vn_ref[0, g]))
        outs.append(o)
        lses.append(m + jnp.log(l))
    o_ref[0] = _merge_groups(lses, outs).astype(o_ref.dtype)


def _attn_decode(q, k_new, v_new, slopes_b, caches):
    n = q.shape[0]
    hpg, e = HEADS_PER_GROUP, HEAD_DIM
    qspec = pl.BlockSpec((1, N_GROUPS, hpg, e), lambda i: (i, 0, 0, 0))
    views, cspecs = [], []
    for g, (_, dil) in enumerate(DIL_GROUPS):
        views.append(caches[g].reshape(n, SPAN, dil, 2, hpg, e))
        cspecs.append(pl.BlockSpec((1, SPAN, None, 2, hpg, e), lambda i: (i, 0, 0, 0, 0, 0)))
    return pl.pallas_call(
        _attn_decode_kernel,
        grid=(n,),
        in_specs=[qspec, qspec, qspec,
                  pl.BlockSpec((N_GROUPS, hpg, LANES), lambda i: (0, 0, 0))] + cspecs,
        out_specs=pl.BlockSpec((1, hpg, e), lambda i: (i, 0, 0)),
        out_shape=jax.ShapeDtypeStruct((n, hpg, e), BF16),
        compiler_params=_cparams("parallel"),
        name="attn_decode",
    )(q, k_new, v_new, slopes_b, *views)


def _cache_shift_kernel(c_ref, nxt_ref, new_ref, o_ref):
    r = c_ref.shape[1]
    o_ref[0, 0:r - 1] = c_ref[0, 1:r]
    last = pl.program_id(1) == pl.num_programs(1) - 1
    o_ref[0, r - 1] = jnp.where(last, new_ref[0], nxt_ref[0, 0])


def _cache_shift(cache, new):
    n, w = cache.shape[:2]
    tail = cache.shape[2:]
    r = min(w, CACHE_ROWS)
    zeros = (0,) * len(tail)
    return pl.pallas_call(
        _cache_shift_kernel,
        grid=(n, w // r),
        in_specs=[pl.BlockSpec((1, r) + tail, lambda b, i: (b, i) + zeros),
                  pl.BlockSpec((1, 1) + tail, lambda b, i: (b, jnp.minimum((i + 1) * r, w - 1)) + zeros),
                  pl.BlockSpec((1,) + tail, lambda b, i: (b,) + zeros)],
        out_specs=pl.BlockSpec((1, r) + tail, lambda b, i: (b, i) + zeros),
        out_shape=jax.ShapeDtypeStruct(cache.shape, cache.dtype),
        compiler_params=_cparams("parallel", "parallel"),
        name="cache_shift",
    )(cache, cache, new)


def _norm_router_kernel(h_ref, g_ref, wr_ref, br_ref, hn_ref, eid_ref, gate_ref):
    h = h_ref[...]
    hn = (h * lax.rsqrt(jnp.mean(h * h, axis=-1, keepdims=True) + EPS)) * g_ref[...]
    hn_ref[...] = hn
    logits = _dot(hn.astype(BF16), wr_ref[...]) + br_ref[...]
    lane = lax.broadcasted_iota(I32, logits.shape, 1)
    ng, epg = N_EXPERT_GROUPS, EXPERTS_PER_GROUP

    def first_argmax(vals, mask):
        v = jnp.where(mask, vals, NEG)
        vmax = jnp.max(v, axis=-1, keepdims=True)
        idx = jnp.min(jnp.where(mask & (v == vmax), lane, 4 * LANES), axis=-1, keepdims=True)
        return vmax, idx

    gmask = lane < ng
    gmax, gsel = first_argmax(logits, gmask)
    p_group = 1.0 / jnp.sum(jnp.where(gmask, jnp.exp(logits - gmax), 0.0), axis=-1, keepdims=True)
    lo = ng + gsel * epg
    emask = (lane >= lo) & (lane < lo + epg)
    v1, i1 = first_argmax(logits, emask)
    v2, i2 = first_argmax(logits, emask & (lane != i1))
    e2 = jnp.exp(v2 - v1)
    g1 = p_group / (1.0 + e2)
    g2 = p_group * e2 / (1.0 + e2)
    eid_ref[...] = jnp.where(lane == 0, i1 - ng, jnp.where(lane == 1, i2 - ng, 0))
    gate_ref[...] = jnp.where(lane == 0, g1, jnp.where(lane == 1, g2, 0.0))


def _norm_router(h, g, w_router, b_router):
    m, d = h.shape
    tm = _row_tile(m, 256)
    row = lambda i: (i, 0)
    fixed = lambda i: (0, 0)
    return pl.pallas_call(
        _norm_router_kernel,
        grid=(m // tm,),
        in_specs=[pl.BlockSpec((tm, d), row),
                  pl.BlockSpec((1, d), fixed),
                  pl.BlockSpec((d, LANES), fixed),
                  pl.BlockSpec((1, LANES), fixed)],
        out_specs=[pl.BlockSpec((tm, d), row),
                   pl.BlockSpec((tm, LANES), row),
                   pl.BlockSpec((tm, LANES), row)],
        out_shape=[jax.ShapeDtypeStruct((m, d), F32),
                   jax.ShapeDtypeStruct((m, LANES), I32),
                   jax.ShapeDtypeStruct((m, LANES), F32)],
        compiler_params=_cparams("parallel"),
        name="norm_router",
    )(h, g.reshape(1, d), w_router, b_router)


def _for_row_groups(rows, fn):
    big = 2 * MOE_GRAN

    def pair(i, _):
        fn(pl.multiple_of(i * big, big), big)
        return 0

    n_pair = rows // big
    lax.fori_loop(0, n_pair, pair, 0)

    @pl.when(rows % big != 0)
    def _():
        fn(pl.multiple_of(n_pair * big, big), MOE_GRAN)


def _zero_tail(ref, rows):
    def fill(i, _):
        ref[pl.ds(pl.multiple_of(i * MOE_GRAN, MOE_GRAN), MOE_GRAN), :] = jnp.zeros(
            (MOE_GRAN, ref.shape[1]), ref.dtype)
        return 0

    lax.fori_loop(rows // MOE_GRAN, ref.shape[0] // MOE_GRAN, fill, 0)


def _moe_up_kernel(sb_exp, sb_rows, row_tok, x_hbm, wg_ref, wu_ref, hb_ref,
                   stage_ref, xb_ref, wgb_ref, wub_ref, sem, *, rmax):
    s = pl.program_id(0)
    f = pl.program_id(1)
    rows = sb_rows[s]
    gran = MOE_GRAN

    @pl.when((rows > 0) & (f == 0))
    def _gather():
        n_groups = rows // gran

        def issue_group(gi, slot):
            base = s * rmax + gi * gran

            def issue(i, _):
                pltpu.make_async_copy(x_hbm.at[pl.ds(row_tok[base + i], 1), :],
                                      stage_ref.at[slot, pl.ds(i, 1), :], sem.at[slot]).start()
                return 0

            lax.fori_loop(0, gran, issue, 0, unroll=8)

        issue_group(0, 0)

        def group(gi, _):
            slot = gi % 2

            @pl.when(gi + 1 < n_groups)
            def _():
                issue_group(gi + 1, 1 - slot)

            pltpu.make_async_copy(x_hbm.at[pl.ds(0, gran), :], stage_ref.at[slot],
                                  sem.at[slot]).wait()
            xb_ref[pl.ds(pl.multiple_of(gi * gran, gran), gran), :] = stage_ref[slot].astype(BF16)
            return 0

        lax.fori_loop(0, n_groups, group, 0)

    @pl.when(rows > 0)
    def _compute():
        wgb_ref[...] = wg_ref[0].astype(BF16)
        wub_ref[...] = wu_ref[0].astype(BF16)

        def sub(r0, size):
            x = xb_ref[pl.ds(r0, size), :]
            gate = _dot(x, wgb_ref[...])
            up = _dot(x, wub_ref[...])
            hb_ref[pl.ds(r0, size), :] = (gate * _sigmoid(gate) * up).astype(hb_ref.dtype)

        _for_row_groups(rows, sub)

    _zero_tail(hb_ref, rows)


def _moe_down_kernel(sb_exp, sb_rows, hb_ref, wd_ref, y_ref):
    rows = sb_rows[pl.program_id(0)]

    @pl.when(rows > 0)
    def _compute():
        wd = wd_ref[0].astype(BF16)

        def sub(r0, size):
            y_ref[pl.ds(r0, size), :] = _dot(hb_ref[pl.ds(r0, size), :], wd)

        _for_row_groups(rows, sub)

    _zero_tail(y_ref, rows)


def _moe_ffn(hn, sb_exp, sb_rows, row_tok, w_gate, w_up, w_down):
    d = hn.shape[1]
    fdim = w_gate.shape[2]
    rmax = MOE_ROWS
    n_sb = sb_exp.shape[0]
    fc = min(fdim, MOE_HIDDEN_CHUNK)
    nf = fdim // fc

    def chunk(r, s, j, last):
        return jnp.where(r[s] > 0, j, last)

    up_spec = pltpu.PrefetchScalarGridSpec(
        num_scalar_prefetch=3,
        grid=(n_sb, nf),
        in_specs=[pl.BlockSpec(memory_space=pl.ANY),
                  pl.BlockSpec((1, d, fc), lambda s, f, e, r, t: (e[s], 0, chunk(r, s, f, nf - 1))),
                  pl.BlockSpec((1, d, fc), lambda s, f, e, r, t: (e[s], 0, chunk(r, s, f, nf - 1)))],
        out_specs=pl.BlockSpec((rmax, fc), lambda s, f, e, r, t: (s, f)),
        scratch_shapes=[pltpu.VMEM((2, MOE_GRAN, d), F32),
                        pltpu.VMEM((rmax, d), BF16),
                        pltpu.VMEM((d, fc), BF16),
                        pltpu.VMEM((d, fc), BF16),
                        pltpu.SemaphoreType.DMA((2,))],
    )
    hb = pl.pallas_call(
        functools.partial(_moe_up_kernel, rmax=rmax),
        grid_spec=up_spec,
        out_shape=jax.ShapeDtypeStruct((n_sb * rmax, fdim), BF16),
        compiler_params=_cparams("arbitrary", "arbitrary"),
        name="moe_up",
    )(sb_exp, sb_rows, row_tok, hn, w_gate, w_up)
    tn = _col_tile(d)
    nn = d // tn
    down_spec = pltpu.PrefetchScalarGridSpec(
        num_scalar_prefetch=2,
        grid=(n_sb, nn),
        in_specs=[pl.BlockSpec((rmax, fdim), lambda s, j, e, r: (s, 0)),
                  pl.BlockSpec((1, fdim, tn), lambda s, j, e, r: (e[s], 0, chunk(r, s, j, nn - 1)))],
        out_specs=pl.BlockSpec((rmax, tn), lambda s, j, e, r: (s, j)),
    )
    return pl.pallas_call(
        _moe_down_kernel,
        grid_spec=down_spec,
        out_shape=jax.ShapeDtypeStruct((n_sb * rmax, d), F32),
        compiler_params=_cparams("arbitrary", "arbitrary"),
        name="moe_down",
    )(sb_exp, sb_rows, hb, w_down)


def _combine_kernel(pos_ref, h_ref, gate_ref, g_ref, yb_hbm, yp_ref, ys_ref, buf_ref, sem, *, nb_p):
    i = pl.program_id(0)
    tb = h_ref.shape[0]
    slot = i % 2

    def issue_block(blk, slot):
        def issue(t, _):
            for k in range(TOP_K):
                row = pos_ref[(blk * tb + t) * TOP_K + k]
                pltpu.make_async_copy(yb_hbm.at[pl.ds(row, 1), :],
                                      buf_ref.at[slot, k, pl.ds(t, 1), :], sem.at[slot]).start()
            return 0

        lax.fori_loop(0, tb, issue, 0, unroll=4)

    @pl.when(i == 0)
    def _():
        issue_block(0, 0)

    @pl.when(i + 1 < pl.num_programs(0))
    def _():
        issue_block(i + 1, 1 - slot)

    for k in range(TOP_K):
        pltpu.make_async_copy(yb_hbm.at[pl.ds(0, tb), :], buf_ref.at[slot, k], sem.at[slot]).wait()
    gate = gate_ref[...]
    y = h_ref[...] + gate[:, 0:1] * buf_ref[slot, 0] + gate[:, 1:2] * buf_ref[slot, 1]
    y = (y * lax.rsqrt(jnp.mean(y * y, axis=-1, keepdims=True) + EPS)) * g_ref[...]

    @pl.when(i < nb_p)
    def _():
        yp_ref[...] = y

    @pl.when(i == nb_p)
    def _():
        ys_ref[...] = y


def _combine(pos, h, gate, g_final, yb, m_p):
    m, d = h.shape
    tb = m - m_p
    assert m_p % tb == 0 and tb % SUBLANES == 0
    nb_p = m_p // tb
    grid_spec = pltpu.PrefetchScalarGridSpec(
        num_scalar_prefetch=1,
        grid=(nb_p + 1,),
        in_specs=[pl.BlockSpec((tb, d), lambda i, p: (i, 0)),
                  pl.BlockSpec((tb, LANES), lambda i, p: (i, 0)),
                  pl.BlockSpec((1, d), lambda i, p: (0, 0)),
                  pl.BlockSpec(memory_space=pl.ANY)],
        out_specs=[pl.BlockSpec((tb, d), lambda i, p: (jnp.minimum(i, nb_p - 1), 0)),
                   pl.BlockSpec((tb, d), lambda i, p: (0, 0))],
        scratch_shapes=[pltpu.VMEM((2, TOP_K, tb, d), F32), pltpu.SemaphoreType.DMA((2,))],
    )
    return pl.pallas_call(
        functools.partial(_combine_kernel, nb_p=nb_p),
        grid_spec=grid_spec,
        out_shape=[jax.ShapeDtypeStruct((m_p, d), F32), jax.ShapeDtypeStruct((tb, d), F32)],
        compiler_params=_cparams("arbitrary"),
        name="combine",
    )(pos, h, gate, g_final.reshape(1, d), yb)


def _dispatch_plan(eid, n_tok):
    a = n_tok * TOP_K
    rmax, gran = MOE_ROWS, MOE_GRAN
    n_sb = -(-a // rmax) + N_EXPERTS
    e_flat = eid.reshape(a)
    onehot = (e_flat[:, None] == jnp.arange(N_EXPERTS, dtype=I32)[None, :]).astype(I32)
    csum = jnp.cumsum(onehot, axis=0)
    counts = csum[-1]
    rank = jnp.take_along_axis(csum, e_flat[:, None], axis=1)[:, 0] - 1
    padded = (counts + gran - 1) // gran * gran
    slabs = (padded + rmax - 1) // rmax
    slab_end = jnp.cumsum(slabs)
    slab_start = slab_end - slabs
    total = slab_end[-1]
    pos = slab_start[e_flat] * rmax + rank
    sidx = jnp.arange(n_sb, dtype=I32)
    s_eff = jnp.minimum(sidx, jnp.maximum(total - 1, 0))
    exp_of = jnp.minimum(jnp.searchsorted(slab_end, s_eff, side="right"), N_EXPERTS - 1).astype(I32)
    local = s_eff - slab_start[exp_of]
    rows = jnp.clip(padded[exp_of] - local * rmax, 0, rmax)
    rows = jnp.where(sidx < total, rows, 0).astype(I32)
    tok = jnp.repeat(jnp.arange(n_tok, dtype=I32), TOP_K)
    row_tok = jnp.zeros((n_sb * rmax,), I32).at[pos].set(tok)
    return exp_of, rows, row_tok, pos.astype(I32)


def kernel(x_prompt, x_sample, cache_kv0, cache_kv1, cache_kv2, state_conv, state_rglru, norm1, w_in,
           conv_w, conv_b, lru_wa, lru_ba, lru_wx, lru_bx, lru_lambda, w_branch_a, w_branch_b, w_out,
           norm2, router_group_w, router_group_b, router_expert_w, router_expert_b, expert_w_gate,
           expert_w_up, expert_w_down, norm_final):
    batch, seq, d = x_prompt.shape
    n_dec, dec_seq, _ = x_sample.shape
    assert w_in.shape[0] == 1 and dec_seq == 1
    hpg, e = HEADS_PER_GROUP, HEAD_DIM
    d_att = N_GROUPS * hpg * e
    q_off, k_off, v_off = 2 * d, 2 * d + d_att, 2 * d + 2 * d_att
    ga_off, gb_off = 2 * d + 3 * d_att, 3 * d + 3 * d_att
    caches = (cache_kv0[0], cache_kv1[0], cache_kv2[0])
    for c, (win, dil) in zip(caches, DIL_GROUPS):
        assert c.shape[1] == win == SPAN * dil and seq % (SPAN * dil) == 0
    m_p = batch * seq
    n_tok = m_p + n_dec
    slopes = 2.0 ** (-8.0 * jnp.arange(1, N_GROUPS * hpg + 1, dtype=F32) / (N_GROUPS * hpg))
    n_pad = LANES - N_EXPERT_GROUPS - N_EXPERTS
    w_router = jnp.concatenate(
        [router_group_w[0], router_expert_w[0], jnp.zeros((d, n_pad), F32)], axis=1).astype(BF16)
    b_router = jnp.concatenate(
        [router_group_b[0], router_expert_b[0], jnp.zeros((n_pad,), F32)]).reshape(1, LANES)
    rnn_w = (conv_w[0], conv_b[0], lru_wa[0], lru_ba[0], lru_wx[0], lru_bx[0], lru_lambda[0])

    x_all = jnp.concatenate([x_prompt.reshape(m_p, d), x_sample.reshape(n_dec, d)], axis=0)
    xn = _rmsnorm(x_all, norm1[0], BF16)
    proj = _matmul(xn, w_in[0], F32)
    proj_s = proj[m_p:]

    a_p, ph = _rglru_prompt(proj, batch, seq, d, *rnn_w)
    o_p = _attn_prompt(proj, slopes, batch, seq, q_off, k_off, v_off)

    conv_t = jnp.transpose(state_conv[0], (1, 0, 2))
    a_s, new_conv_t, new_h = _rglru_step(proj, m_p, n_dec, conv_t, state_rglru[0], d, *rnn_w)
    heads = lambda off: proj_s[:, off:off + d_att].reshape(n_dec, N_GROUPS, hpg, e)
    q_s, k_s, v_s = heads(q_off), heads(k_off), heads(v_off)
    slopes_b = jnp.broadcast_to(slopes.reshape(N_GROUPS, hpg, 1), (N_GROUPS, hpg, LANES))
    o_s = _attn_decode(q_s, k_s, v_s, slopes_b, caches).reshape(n_dec, hpg * e)
    kv_new = [jnp.stack([k_s[:, g], v_s[:, g]], axis=1) for g in range(N_GROUPS)]
    s_kv = [_cache_shift(c, new) for c, new in zip(caches, kv_new)]

    a_all = jnp.concatenate([a_p, a_s], axis=0)
    o_all = jnp.concatenate([o_p, o_s], axis=0)
    u = _merge(a_all, o_all, w_branch_a[0], w_branch_b[0], proj, ga_off, gb_off)
    h = _outproj(x_all, u, w_out[0])

    hn, eid, gate = _norm_router(h, norm2[0], w_router, b_router)
    sb_exp, sb_rows, row_tok, pos = _dispatch_plan(eid[:, :TOP_K], n_tok)
    yb = _moe_ffn(hn, sb_exp, sb_rows, row_tok, expert_w_gate[0], expert_w_up[0], expert_w_down[0])
    y_p, y_s = _combine(pos, h, gate, norm_final, yb, m_p)

    def heads_p(off, g):
        cols = proj[:m_p, off + g * hpg * e:off + (g + 1) * hpg * e]
        return cols.reshape(batch, seq, hpg, e)

    p_kv = []
    for g, (win, _) in enumerate(DIL_GROUPS):
        w = min(win, seq)
        p_kv.append(jnp.stack([heads_p(k_off, g)[:, -w:], heads_p(v_off, g)[:, -w:]], axis=2)[None])
    p_conv = proj[:m_p, :d].reshape(batch, seq, d)[:, -(CONV_W - 1):][None]
    p_h = ph.reshape(batch, d)[None]
    s_conv = jnp.transpose(new_conv_t, (1, 0, 2))[None]
    return (y_p.reshape(batch, seq, d), y_s.reshape(n_dec, 1, d), p_kv[0], p_kv[1], p_kv[2],
            p_conv, p_h, s_kv[0][None], s_kv[1][None], s_kv[2][None], s_conv, new_h[None])
```

```python
import functools

import jax
import jax.numpy as jnp
from jax import lax
from jax.experimental import pallas as pl
from jax.experimental.pallas import tpu as pltpu

F32, BF16, I32 = jnp.float32, jnp.bfloat16, jnp.int32

N_RNN_BLOCKS = 16
CONV_W = 4
LRU_C = 8.0
HEAD_DIM = 128
HEADS_PER_GROUP = 8
DIL_GROUPS = ((128, 1), (512, 4), (2048, 16))
N_GROUPS = len(DIL_GROUPS)
SPAN = 128
N_EXPERT_GROUPS = 4
EXPERTS_PER_GROUP = 8
N_EXPERTS = N_EXPERT_GROUPS * EXPERTS_PER_GROUP
TOP_K = 2
EPS = 1e-6
PAST_LEN = 2048
ATT_SCALE = HEAD_DIM ** -0.5
NEG = -1e30

LANES = 128
SUBLANES = 8
VMEM_LIMIT = 56 * 1024 * 1024
ROW_TILE = 1024
COL_TILE = 512
MOE_ROWS = 1024
MOE_GRAN = 128
MOE_HIDDEN_CHUNK = 256
SCAN_CHUNKS = 16
SCAN_PITCH_PAD = 8
CACHE_ROWS = 512
MM_COPY_ROWS = 256
MM_COPY_CHUNKS = 4
CAST_BLOCK_BYTES = 10 * 1024 * 1024
ATTN_UNROLL = 4


def _cparams(*sem):
    return pltpu.CompilerParams(dimension_semantics=sem, vmem_limit_bytes=VMEM_LIMIT)


def _bf16_round(x):
    return x.astype(BF16).astype(F32)


def _sigmoid(x):
    return 1.0 / (1.0 + jnp.exp(-x))


def _gelu_tanh(x):
    return 0.5 * x * (1.0 + jnp.tanh(0.7978845608028654 * (x + 0.044715 * (x * x * x))))


def _one_minus_exp2x(x):
    t = jnp.tanh(x)
    return (-2.0 * t) / (1.0 - t)


def _softplus(x):
    return jnp.maximum(x, 0.0) + jnp.log1p(jnp.exp(-jnp.abs(x)))


def _dot(a, b):
    return jnp.dot(a, b, preferred_element_type=F32)


def _col_tile(n, *offsets):
    t = COL_TILE
    while t > LANES and (n % t or any(o % t for o in offsets)):
        t //= 2
    return t


def _row_tile(m, pref=ROW_TILE):
    best = None
    for t in range(LANES, min(m, pref) + 1, LANES):
        if m % t == 0:
            best = t
    return best or m


def _rmsnorm_kernel(x_ref, g_ref, o_ref):
    x = x_ref[...]
    y = x * lax.rsqrt(jnp.mean(x * x, axis=-1, keepdims=True) + EPS)
    o_ref[...] = (y * g_ref[...]).astype(o_ref.dtype)


def _rmsnorm(x, g, out_dtype):
    m, d = x.shape
    tm = _row_tile(m, 256)
    return pl.pallas_call(
        _rmsnorm_kernel,
        grid=(m // tm,),
        in_specs=[pl.BlockSpec((tm, d), lambda i: (i, 0)),
                  pl.BlockSpec((1, d), lambda i: (0, 0))],
        out_specs=pl.BlockSpec((tm, d), lambda i: (i, 0)),
        out_shape=jax.ShapeDtypeStruct((m, d), out_dtype),
        compiler_params=_cparams("parallel"),
        name="rmsnorm",
    )(x, g.reshape(1, d))


def _cast_kernel(w_ref, o_ref):
    o_ref[...] = w_ref[...].astype(o_ref.dtype)


def _to_bf16(w):
    k, n = w.shape
    tk = _row_tile(k)
    tn = max(t for t in range(LANES, n + 1, LANES) if n % t == 0 and tk * t * 4 <= CAST_BLOCK_BYTES)
    return pl.pallas_call(
        _cast_kernel,
        grid=(k // tk, n // tn),
        in_specs=[pl.BlockSpec((tk, tn), lambda i, j: (i, j))],
        out_specs=pl.BlockSpec((tk, tn), lambda i, j: (i, j)),
        out_shape=jax.ShapeDtypeStruct((k, n), BF16),
        compiler_params=_cparams("parallel", "parallel"),
        name="to_bf16",
    )(w)


def _mm_copy_kernel(x_ref, w_ref, cache_hbm, o_ref, cout_hbm, ring_ref, sem_in, sem_out, *,
                    copy_steps, cps, rows):
    t = pl.program_id(0) * pl.num_programs(1) + pl.program_id(1)
    w_rows = cache_hbm.shape[1]
    cpn = w_rows // rows

    def for_chunks(step, phase):
        slot = step % 2
        for q in range(cps):
            c = step * cps + q
            n = c // cpn
            r0 = (c % cpn) * rows
            buf = ring_ref.at[slot, q]
            if phase in ("read", "write"):
                body = pltpu.make_async_copy(cache_hbm.at[n, pl.ds(r0 + 1, rows - 1)],
                                             buf.at[pl.ds(0, rows - 1)], sem_in.at[slot, q])
                tail = pltpu.make_async_copy(
                    cache_hbm.at[n, pl.ds(jnp.minimum(r0 + rows, w_rows - 1), 1)],
                    buf.at[pl.ds(rows - 1, 1)], sem_in.at[slot, q])
            if phase in ("write", "done"):
                dst = pltpu.make_async_copy(buf, cout_hbm.at[n, pl.ds(r0, rows)], sem_out.at[slot, q])
            if phase == "read":
                body.start()
                tail.start()
            elif phase == "write":
                body.wait()
                tail.wait()
                dst.start()
            else:
                dst.wait()

    @pl.when((t >= 2) & (t < copy_steps + 2))
    def _():
        for_chunks(t - 2, "done")

    @pl.when(t < copy_steps)
    def _():
        for_chunks(t, "read")

    @pl.when((t >= 1) & (t < copy_steps + 1))
    def _():
        for_chunks(t - 1, "write")

    o_ref[...] = _dot(x_ref[...], w_ref[...]).astype(o_ref.dtype)


def _matmul_and_shift(x, w, cache):
    m, k = x.shape
    n = w.shape[1]
    tm = _row_tile(m)
    tn = 2 * COL_TILE if n % (2 * COL_TILE) == 0 else _col_tile(n)
    n_seq, w_rows = cache.shape[:2]
    rows, cps = MM_COPY_ROWS, MM_COPY_CHUNKS
    n_chunks = n_seq * (w_rows // rows)
    assert w_rows % rows == 0 and n_chunks % cps == 0
    copy_steps = n_chunks // cps
    steps = (n // tn) * (m // tm)
    assert copy_steps + 2 <= steps, (copy_steps, steps)
    return pl.pallas_call(
        functools.partial(_mm_copy_kernel, copy_steps=copy_steps, cps=cps, rows=rows),
        grid=(n // tn, m // tm),
        in_specs=[pl.BlockSpec((tm, k), lambda j, i: (i, 0)),
                  pl.BlockSpec((k, tn), lambda j, i: (0, j)),
                  pl.BlockSpec(memory_space=pl.ANY)],
        out_specs=[pl.BlockSpec((tm, tn), lambda j, i: (i, j)),
                   pl.BlockSpec(memory_space=pl.ANY)],
        out_shape=[jax.ShapeDtypeStruct((m, n), F32),
                   jax.ShapeDtypeStruct(cache.shape, cache.dtype)],
        scratch_shapes=[pltpu.VMEM((2, cps, rows) + cache.shape[2:], cache.dtype),
                        pltpu.SemaphoreType.DMA((2, cps)),
                        pltpu.SemaphoreType.DMA((2, cps))],
        compiler_params=_cparams("arbitrary", "arbitrary"),
        name="matmul_shift",
    )(x, w, cache)


def _set_last_row_kernel(c_hbm, new_ref, o_hbm, sem):
    del c_hbm
    cp = pltpu.make_async_copy(new_ref, o_hbm.at[:, o_hbm.shape[1] - 1], sem)
    cp.start()
    cp.wait()


def _set_last_row(cache, new):
    return pl.pallas_call(
        _set_last_row_kernel,
        in_specs=[pl.BlockSpec(memory_space=pl.ANY), pl.BlockSpec(memory_space=pltpu.VMEM)],
        out_specs=pl.BlockSpec(memory_space=pl.ANY),
        out_shape=jax.ShapeDtypeStruct(cache.shape, cache.dtype),
        scratch_shapes=[pltpu.SemaphoreType.DMA(())],
        input_output_aliases={0: 0},
        name="set_last_row",
    )(cache, new)


def _merge_kernel(a_ref, o_ref, wa_ref, wb_ref, ga_ref, gb_ref, u_ref, wab_ref, wbb_ref):
    @pl.when(pl.program_id(1) == 0)
    def _():
        wab_ref[...] = wa_ref[...].astype(BF16)
        wbb_ref[...] = wb_ref[...].astype(BF16)

    ya = _dot(a_ref[...], wab_ref[...])
    yb = _dot(o_ref[...], wbb_ref[...])
    u = _sigmoid(ga_ref[...]) * ya + _sigmoid(gb_ref[...]) * yb
    u_ref[...] = u.astype(u_ref.dtype)


def _merge(a, o, w_a, w_b, proj, ga_off, gb_off):
    m, ka = a.shape
    kb = o.shape[1]
    n = w_a.shape[1]
    tm = _row_tile(m)
    tn = _col_tile(n, ga_off, gb_off)
    return pl.pallas_call(
        _merge_kernel,
        grid=(n // tn, m // tm),
        in_specs=[pl.BlockSpec((tm, ka), lambda j, i: (i, 0)),
                  pl.BlockSpec((tm, kb), lambda j, i: (i, 0)),
                  pl.BlockSpec((ka, tn), lambda j, i: (0, j)),
                  pl.BlockSpec((kb, tn), lambda j, i: (0, j)),
                  pl.BlockSpec((tm, tn), lambda j, i: (i, j + ga_off // tn)),
                  pl.BlockSpec((tm, tn), lambda j, i: (i, j + gb_off // tn))],
        out_specs=pl.BlockSpec((tm, tn), lambda j, i: (i, j)),
        out_shape=jax.ShapeDtypeStruct((m, n), BF16),
        scratch_shapes=[pltpu.VMEM((ka, tn), BF16), pltpu.VMEM((kb, tn), BF16)],
        compiler_params=_cparams("parallel", "arbitrary"),
        name="merge",
    )(a, o, w_a, w_b, proj, proj)


def _outproj_kernel(x_ref, u_ref, w_ref, h_ref, wb_ref):
    @pl.when(pl.program_id(1) == 0)
    def _():
        wb_ref[...] = w_ref[...].astype(BF16)

    h_ref[...] = x_ref[...] + _dot(u_ref[...], wb_ref[...])


def _outproj(x, u, w):
    m, k = u.shape
    n = w.shape[1]
    tm = _row_tile(m)
    tn = _col_tile(n)
    return pl.pallas_call(
        _outproj_kernel,
        grid=(n // tn, m // tm),
        in_specs=[pl.BlockSpec((tm, tn), lambda j, i: (i, j)),
                  pl.BlockSpec((tm, k), lambda j, i: (i, 0)),
                  pl.BlockSpec((k, tn), lambda j, i: (0, j))],
        out_specs=pl.BlockSpec((tm, tn), lambda j, i: (i, j)),
        out_shape=jax.ShapeDtypeStruct((m, n), F32),
        scratch_shapes=[pltpu.VMEM((k, tn), BF16)],
        compiler_params=_cparams("parallel", "arbitrary"),
        name="outproj",
    )(x, u, w)


def _rglru_prompt_kernel(xr_ref, yr_ref, cw_ref, cb_ref, wa_ref, ba_ref, wx_ref, bx_ref, lam_ref,
                         a_out_ref, ph_ref,
                         xe_ref, a_ref, u_ref, hs_ref, as_ref, hsz_ref, *, t_len, nc):
    c = xr_ref.shape[1]
    nl = c // LANES
    cl = t_len // nc
    pitch = cl + SCAN_PITCH_PAD
    xe_ref[0:SUBLANES, :] = jnp.zeros((SUBLANES, c), F32)
    xe_ref[SUBLANES:SUBLANES + t_len, :] = _bf16_round(xr_ref[...])
    wa = wa_ref[0].astype(BF16)
    wx = wx_ref[0].astype(BF16)
    decay = (-LRU_C) * _softplus(-lam_ref[...])
    cw = _bf16_round(cw_ref[...])
    for ch in range(nc):
        r0 = ch * cl
        xc = cw[0:1] * xe_ref[r0 + 5:r0 + 5 + cl, :]
        for j in range(1, CONV_W):
            xc = xc + cw[j:j + 1] * xe_ref[r0 + 5 + j:r0 + 5 + j + cl, :]
        xc = xc + cb_ref[...]
        xcb = xc.astype(BF16)
        r = _sigmoid(_dot(xcb, wa) + ba_ref[...])
        i = _sigmoid(_dot(xcb, wx) + bx_ref[...])
        log_a = r * decay
        a = jnp.exp(log_a)
        mult = jnp.sqrt(_one_minus_exp2x(log_a))
        if ch == 0:
            row = lax.broadcasted_iota(I32, (cl, c), 0)
            a = jnp.where(row == 0, 0.0, a)
            mult = jnp.where(row == 0, 1.0, mult)
        au = (a, mult * (i * xc))
        for lb in range(nl):
            for ref, val in zip((a_ref, u_ref), au):
                ref[lb, ch * pitch:ch * pitch + cl, :] = val[:, lb * LANES:(lb + 1) * LANES]

    def pass1(j, carry):
        out = []
        for lb in range(nl):
            acc_a, acc_h = carry[2 * lb], carry[2 * lb + 1]
            a = a_ref[lb, pl.ds(j, nc, stride=pitch), :]
            u = u_ref[lb, pl.ds(j, nc, stride=pitch), :]
            acc_a = a * acc_a
            acc_h = a * acc_h + u
            as_ref[lb, j] = acc_a
            hsz_ref[lb, j] = acc_h
            out += [acc_a, acc_h]
        return tuple(out)

    init = (jnp.ones((nc, LANES), F32), jnp.zeros((nc, LANES), F32)) * nl
    fin = lax.fori_loop(0, cl, pass1, init)
    cins = []
    for lb in range(nl):
        fin_a, fin_h = fin[2 * lb], fin[2 * lb + 1]
        carry = jnp.zeros((1, LANES), F32)
        rows = []
        for ch in range(nc):
            rows.append(carry)
            carry = fin_a[ch:ch + 1] * carry + fin_h[ch:ch + 1]
        cins.append(jnp.concatenate(rows, axis=0))
        ph_ref[0, :, lb * LANES:(lb + 1) * LANES] = carry

    def pass2(j, _):
        for lb in range(nl):
            hs_ref[lb, pl.ds(j, nc, stride=pitch), :] = hsz_ref[lb, j] + as_ref[lb, j] * cins[lb]
        return 0

    lax.fori_loop(0, cl, pass2, 0)
    for ch in range(nc):
        for lb in range(nl):
            y = yr_ref[ch * cl:(ch + 1) * cl, lb * LANES:(lb + 1) * LANES]
            a_out_ref[ch * cl:(ch + 1) * cl, lb * LANES:(lb + 1) * LANES] = (
                _gelu_tanh(y) * hs_ref[lb, ch * pitch:ch * pitch + cl, :]).astype(a_out_ref.dtype)


def _rglru_prompt(proj, batch, t_len, d, conv_w, conv_b, lru_wa, lru_ba, lru_wx, lru_bx, lam):
    nb = lru_wa.shape[0]
    c = d // nb
    nl = c // LANES
    nc = SCAN_CHUNKS
    cl = t_len // nc
    pitch = cl + SCAN_PITCH_PAD
    vec = lambda v: v.reshape(1, d)
    vspec = pl.BlockSpec((1, c), lambda b, n: (0, n))
    wspec = pl.BlockSpec((1, c, c), lambda b, n: (n, 0, 0))
    return pl.pallas_call(
        functools.partial(_rglru_prompt_kernel, t_len=t_len, nc=nc),
        grid=(batch, nb),
        in_specs=[pl.BlockSpec((t_len, c), lambda b, n: (b, n)),
                  pl.BlockSpec((t_len, c), lambda b, n: (b, n + nb)),
                  pl.BlockSpec((CONV_W, c), lambda b, n: (0, n)),
                  vspec, wspec, vspec, wspec, vspec, vspec],
        out_specs=[pl.BlockSpec((t_len, c), lambda b, n: (b, n)),
                   pl.BlockSpec((1, 1, c), lambda b, n: (b, 0, n))],
        out_shape=[jax.ShapeDtypeStruct((batch * t_len, d), BF16),
                   jax.ShapeDtypeStruct((batch, 1, d), F32)],
        scratch_shapes=[pltpu.VMEM((t_len + SUBLANES, c), F32),
                        pltpu.VMEM((nl, nc * pitch, LANES), F32),
                        pltpu.VMEM((nl, nc * pitch, LANES), F32),
                        pltpu.VMEM((nl, nc * pitch, LANES), F32),
                        pltpu.VMEM((nl, cl, nc, LANES), F32),
                        pltpu.VMEM((nl, cl, nc, LANES), F32)],
        compiler_params=_cparams("parallel", "parallel"),
        name="rglru_prompt",
    )(proj, proj, conv_w, vec(conv_b), lru_wa, vec(lru_ba), lru_wx, vec(lru_bx), vec(lam))


def _rglru_step_kernel(xr_ref, yr_ref, cs_ref, h0_ref, cw_ref, cb_ref, wa_ref, ba_ref, wx_ref, bx_ref,
                       lam_ref, a_out_ref, ncs_ref, nh_ref, *, reset):
    x = xr_ref[...]
    cw = cw_ref[...]
    xc = cw[0:1] * cs_ref[0]
    for j in range(1, CONV_W - 1):
        xc = xc + cw[j:j + 1] * cs_ref[j]
    xc = xc + cw[CONV_W - 1:CONV_W] * x + cb_ref[...]
    xcb = xc.astype(BF16)
    r = _sigmoid(_dot(xcb, wa_ref[0].astype(BF16)) + ba_ref[...])
    i = _sigmoid(_dot(xcb, wx_ref[0].astype(BF16)) + bx_ref[...])
    log_a = r * ((-LRU_C) * _softplus(-lam_ref[...]))
    if reset:
        h = i * xc
    else:
        h = jnp.exp(log_a) * h0_ref[...] + jnp.sqrt(_one_minus_exp2x(log_a)) * (i * xc)
    nh_ref[...] = h
    a_out_ref[...] = (_gelu_tanh(yr_ref[...]) * h).astype(a_out_ref.dtype)
    for j in range(CONV_W - 2):
        ncs_ref[j] = cs_ref[j + 1]
    ncs_ref[CONV_W - 2] = x


def _rglru_step(proj, row_off, n, conv_state, h0, d, conv_w, conv_b, lru_wa, lru_ba, lru_wx, lru_bx,
                lam):
    nb = lru_wa.shape[0]
    c = d // nb
    assert row_off % n == 0
    rb = row_off // n
    vec = lambda v: v.reshape(1, d)
    vspec = pl.BlockSpec((1, c), lambda j: (0, j))
    wspec = pl.BlockSpec((1, c, c), lambda j: (j, 0, 0))
    xspec = pl.BlockSpec((n, c), lambda j: (0, j))
    sspec = pl.BlockSpec((CONV_W - 1, n, c), lambda j: (0, 0, j))
    return pl.pallas_call(
        functools.partial(_rglru_step_kernel, reset=(PAST_LEN == 0)),
        grid=(nb,),
        in_specs=[pl.BlockSpec((n, c), lambda j: (rb, j)),
                  pl.BlockSpec((n, c), lambda j: (rb, j + nb)), sspec, xspec,
                  pl.BlockSpec((CONV_W, c), lambda j: (0, j)),
                  vspec, wspec, vspec, wspec, vspec, vspec],
        out_specs=[xspec, sspec, xspec],
        out_shape=[jax.ShapeDtypeStruct((n, d), BF16),
                   jax.ShapeDtypeStruct((CONV_W - 1, n, d), F32),
                   jax.ShapeDtypeStruct((n, d), F32)],
        compiler_params=_cparams("parallel"),
        name="rglru_step",
    )(proj, proj, conv_state, h0, conv_w, vec(conv_b), lru_wa, vec(lru_ba), lru_wx, vec(lru_bx),
      vec(lam))


def _unroll_for(trips):
    return max(u for u in range(1, ATTN_UNROLL + 1) if trips % u == 0)


def _merge_groups(lses, outs):
    m = functools.reduce(jnp.maximum, lses)
    es = [jnp.exp(x - m) for x in lses]
    den = functools.reduce(lambda a, b: a + b, es)
    acc = _bf16_round(es[0] / den) * _bf16_round(outs[0])
    for g in range(1, len(outs)):
        acc = acc + _bf16_round(es[g] / den) * _bf16_round(outs[g])
    return acc


def _attn_prompt_kernel(slope_ref, *refs, t_len):
    qkv = refs[:3 * N_GROUPS]
    o_ref = refs[3 * N_GROUPS]
    og_ref, lse_ref = refs[3 * N_GROUPS + 1:]
    h = pl.program_id(1)

    for g, (_, dil) in enumerate(DIL_GROUPS):
        q_ref, k_ref, v_ref = qkv[3 * g:3 * g + 3]
        slope = slope_ref[g * HEADS_PER_GROUP + h] * float(dil)
        n_blk = t_len // dil // SPAN

        def rows(ref, start, size, dil=dil):
            if dil == 1:
                return ref[pl.ds(start, size), :]
            return ref[pl.ds(start, size, stride=dil), :]

        def put(ref, g, start, val, dil=dil):
            if dil == 1:
                ref[g, pl.ds(start, SPAN), :] = val
            else:
                ref[g, pl.ds(start, SPAN, stride=dil), :] = val

        def block(base, with_prev, g=g, dil=dil, slope=slope, q_ref=q_ref, k_ref=k_ref,
                  v_ref=v_ref, rows=rows, put=put):
            nk = 2 * SPAN if with_prev else SPAN
            k0 = base - SPAN * dil if with_prev else base
            q = (rows(q_ref, base, SPAN) * ATT_SCALE).astype(BF16)
            kk = rows(k_ref, k0, nk).astype(BF16)
            vv = rows(v_ref, k0, nk).astype(BF16)
            s = lax.dot_general(q, kk, (((1,), (1,)), ((), ())), preferred_element_type=F32)
            qi = lax.broadcasted_iota(I32, (SPAN, nk), 0)
            kj = lax.broadcasted_iota(I32, (SPAN, nk), 1)
            diff = qi - kj + (SPAN if with_prev else 0)
            valid = (diff >= 0) & (diff <= SPAN)
            s = jnp.where(valid, s - slope * diff.astype(F32), NEG)
            m = jnp.max(s, axis=-1, keepdims=True)
            p = jnp.exp(s - m)
            l = jnp.sum(p, axis=-1, keepdims=True)
            put(og_ref, g, base, _dot((p / l).astype(BF16), vv))
            put(lse_ref, g, base, jnp.broadcast_to(m + jnp.log(l), (SPAN, LANES)))

        def per_class(r, _, block=block, n_blk=n_blk, dil=dil):
            block(r, False)

            def later(n, _):
                block(n * (SPAN * dil) + r, True)
                return 0

            if n_blk > 1:
                lax.fori_loop(1, n_blk, later, 0, unroll=_unroll_for(n_blk - 1))
            return 0

        if dil == 1:
            per_class(0, 0)
        else:
            lax.fori_loop(0, dil, per_class, 0, unroll=_unroll_for(dil) if n_blk == 1 else 1)

    rc = 256
    for c0 in range(0, t_len, rc):
        acc = _merge_groups([lse_ref[g, c0:c0 + rc, :] for g in range(N_GROUPS)],
                            [og_ref[g, c0:c0 + rc, :] for g in range(N_GROUPS)])
        o_ref[c0:c0 + rc, :] = acc.astype(o_ref.dtype)


def _attn_prompt(proj, slopes, batch, t_len, q_off, k_off, v_off):
    hpg, e = HEADS_PER_GROUP, HEAD_DIM
    in_specs = []
    for g in range(N_GROUPS):
        for off in (q_off, k_off, v_off):
            cb = off // e + g * hpg
            in_specs.append(pl.BlockSpec((t_len, e), lambda b, h, s, cb=cb: (b, cb + h)))
    grid_spec = pltpu.PrefetchScalarGridSpec(
        num_scalar_prefetch=1,
        grid=(batch, hpg),
        in_specs=in_specs,
        out_specs=pl.BlockSpec((t_len, e), lambda b, h, s: (b, h)),
        scratch_shapes=[pltpu.VMEM((N_GROUPS, t_len, e), F32),
                        pltpu.VMEM((N_GROUPS, t_len, LANES), F32)],
    )
    return pl.pallas_call(
        functools.partial(_attn_prompt_kernel, t_len=t_len),
        grid_spec=grid_spec,
        out_shape=jax.ShapeDtypeStruct((batch * t_len, hpg * e), BF16),
        compiler_params=_cparams("parallel", "parallel"),
        name="attn_prompt",
    )(slopes, *([proj] * (3 * N_GROUPS)))


def _attn_decode_kernel(q_ref, kn_ref, vn_ref, sl_ref, c0_ref, c1_ref, c2_ref, o_ref):
    caches = (c0_ref, c1_ref, c2_ref)
    outs, lses = [], []
    for g, (_, dil) in enumerate(DIL_GROUPS):
        cache = caches[g]
        q = _bf16_round(q_ref[0, g] * ATT_SCALE)
        kk = _bf16_round(cache[0, :, 0])
        vv = _bf16_round(cache[0, :, 1])
        slope = sl_ref[g][:, 0:1]
        s = jnp.sum(kk * q[None], axis=-1, keepdims=True)
        j = lax.broadcasted_iota(I32, s.shape, 0)
        s = s - slope[None] * ((SPAN - j) * dil).astype(F32)
        s_new = jnp.sum(_bf16_round(kn_ref[0, g]) * q, axis=-1, keepdims=True)
        m = jnp.maximum(jnp.max(s, axis=0), s_new)
        p = jnp.exp(s - m[None])
        p_new = jnp.exp(s_new - m)
        l = jnp.sum(p, axis=0) + p_new
        o = (jnp.sum(_bf16_round(p / l[None]) * vv, axis=0)
             + _bf16_round(p_new / l) * _bf16_round(vn_ref[0, g]))
        outs.append(o)
        lses.append(m + jnp.log(l))
    o_ref[0] = _merge_groups(lses, outs).astype(o_ref.dtype)


def _attn_decode(q, k_new, v_new, slopes_b, caches):
    n = q.shape[0]
    hpg, e = HEADS_PER_GROUP, HEAD_DIM
    qspec = pl.BlockSpec((1, N_GROUPS, hpg, e), lambda i: (i, 0, 0, 0))
    views, cspecs = [], []
    for g, (_, dil) in enumerate(DIL_GROUPS):
        views.append(caches[g].reshape(n, SPAN, dil, 2, hpg, e))
        cspecs.append(pl.BlockSpec((1, SPAN, None, 2, hpg, e), lambda i: (i, 0, 0, 0, 0, 0)))
    return pl.pallas_call(
        _attn_decode_kernel,
        grid=(n,),
        in_specs=[qspec, qspec, qspec,
                  pl.BlockSpec((N_GROUPS, hpg, LANES), lambda i: (0, 0, 0))] + cspecs,
        out_specs=pl.BlockSpec((1, hpg, e), lambda i: (i, 0, 0)),
        out_shape=jax.ShapeDtypeStruct((n, hpg, e), BF16),
        compiler_params=_cparams("parallel"),
        name="attn_decode",
    )(q, k_new, v_new, slopes_b, *views)


def _cache_shift_kernel(c_ref, nxt_ref, new_ref, o_ref):
    r = c_ref.shape[1]
    o_ref[0, 0:r - 1] = c_ref[0, 1:r]
    last = pl.program_id(1) == pl.num_programs(1) - 1
    o_ref[0, r - 1] = jnp.where(last, new_ref[0], nxt_ref[0, 0])


def _cache_shift(cache, new):
    n, w = cache.shape[:2]
    tail = cache.shape[2:]
    r = min(w, CACHE_ROWS)
    zeros = (0,) * len(tail)
    return pl.pallas_call(
        _cache_shift_kernel,
        grid=(n, w // r),
        in_specs=[pl.BlockSpec((1, r) + tail, lambda b, i: (b, i) + zeros),
                  pl.BlockSpec((1, 1) + tail, lambda b, i: (b, jnp.minimum((i + 1) * r, w - 1)) + zeros),
                  pl.BlockSpec((1,) + tail, lambda b, i: (b,) + zeros)],
        out_specs=pl.BlockSpec((1, r) + tail, lambda b, i: (b, i) + zeros),
        out_shape=jax.ShapeDtypeStruct(cache.shape, cache.dtype),
        compiler_params=_cparams("parallel", "parallel"),
        name="cache_shift",
    )(cache, cache, new)


def _norm_router_kernel(h_ref, g_ref, wr_ref, br_ref, hn_ref, eid_ref, gate_ref):
    h = h_ref[...]
    hn = (h * lax.rsqrt(jnp.mean(h * h, axis=-1, keepdims=True) + EPS)) * g_ref[...]
    hn_ref[...] = hn
    logits = _dot(hn.astype(BF16), wr_ref[...]) + br_ref[...]
    lane = lax.broadcasted_iota(I32, logits.shape, 1)
    ng, epg = N_EXPERT_GROUPS, EXPERTS_PER_GROUP

    def first_argmax(vals, mask):
        v = jnp.where(mask, vals, NEG)
        vmax = jnp.max(v, axis=-1, keepdims=True)
        idx = jnp.min(jnp.where(mask & (v == vmax), lane, 4 * LANES), axis=-1, keepdims=True)
        return vmax, idx

    gmask = lane < ng
    gmax, gsel = first_argmax(logits, gmask)
    p_group = 1.0 / jnp.sum(jnp.where(gmask, jnp.exp(logits - gmax), 0.0), axis=-1, keepdims=True)
    lo = ng + gsel * epg
    emask = (lane >= lo) & (lane < lo + epg)
    v1, i1 = first_argmax(logits, emask)
    v2, i2 = first_argmax(logits, emask & (lane != i1))
    e2 = jnp.exp(v2 - v1)
    g1 = p_group / (1.0 + e2)
    g2 = p_group * e2 / (1.0 + e2)
    eid_ref[...] = jnp.where(lane == 0, i1 - ng, jnp.where(lane == 1, i2 - ng, 0))
    gate_ref[...] = jnp.where(lane == 0, g1, jnp.where(lane == 1, g2, 0.0))


def _norm_router(h, g, w_router, b_router):
    m, d = h.shape
    tm = _row_tile(m, 256)
    row = lambda i: (i, 0)
    fixed = lambda i: (0, 0)
    return pl.pallas_call(
        _norm_router_kernel,
        grid=(m // tm,),
        in_specs=[pl.BlockSpec((tm, d), row),
                  pl.BlockSpec((1, d), fixed),
                  pl.BlockSpec((d, LANES), fixed),
                  pl.BlockSpec((1, LANES), fixed)],
        out_specs=[pl.BlockSpec((tm, d), row),
                   pl.BlockSpec((tm, LANES), row),
                   pl.BlockSpec((tm, LANES), row)],
        out_shape=[jax.ShapeDtypeStruct((m, d), F32),
                   jax.ShapeDtypeStruct((m, LANES), I32),
                   jax.ShapeDtypeStruct((m, LANES), F32)],
        compiler_params=_cparams("parallel"),
        name="norm_router",
    )(h, g.reshape(1, d), w_router, b_router)


def _for_row_groups(rows, fn):
    big = 2 * MOE_GRAN

    def pair(i, _):
        fn(pl.multiple_of(i * big, big), big)
        return 0

    n_pair = rows // big
    lax.fori_loop(0, n_pair, pair, 0)

    @pl.when(rows % big != 0)
    def _():
        fn(pl.multiple_of(n_pair * big, big), MOE_GRAN)


def _zero_tail(ref, rows):
    def fill(i, _):
        ref[pl.ds(pl.multiple_of(i * MOE_GRAN, MOE_GRAN), MOE_GRAN), :] = jnp.zeros(
            (MOE_GRAN, ref.shape[1]), ref.dtype)
        return 0

    lax.fori_loop(rows // MOE_GRAN, ref.shape[0] // MOE_GRAN, fill, 0)


def _moe_up_kernel(sb_exp, sb_rows, row_tok, x_hbm, wg_ref, wu_ref, hb_ref,
                   stage_ref, xb_ref, wgb_ref, wub_ref, sem, *, rmax):
    s = pl.program_id(0)
    f = pl.program_id(1)
    rows = sb_rows[s]
    gran = MOE_GRAN

    @pl.when((rows > 0) & (f == 0))
    def _gather():
        n_groups = rows // gran

        def issue_group(gi, slot):
            base = s * rmax + gi * gran

            def issue(i, _):
                pltpu.make_async_copy(x_hbm.at[pl.ds(row_tok[base + i], 1), :],
                                      stage_ref.at[slot, pl.ds(i, 1), :], sem.at[slot]).start()
                return 0

            lax.fori_loop(0, gran, issue, 0, unroll=8)

        issue_group(0, 0)

        def group(gi, _):
            slot = gi % 2

            @pl.when(gi + 1 < n_groups)
            def _():
                issue_group(gi + 1, 1 - slot)

            pltpu.make_async_copy(x_hbm.at[pl.ds(0, gran), :], stage_ref.at[slot],
                                  sem.at[slot]).wait()
            xb_ref[pl.ds(pl.multiple_of(gi * gran, gran), gran), :] = stage_ref[slot].astype(BF16)
            return 0

        lax.fori_loop(0, n_groups, group, 0)

    @pl.when(rows > 0)
    def _compute():
        wgb_ref[...] = wg_ref[0].astype(BF16)
        wub_ref[...] = wu_ref[0].astype(BF16)

        def sub(r0, size):
            x = xb_ref[pl.ds(r0, size), :]
            gate = _dot(x, wgb_ref[...])
            up = _dot(x, wub_ref[...])
            hb_ref[pl.ds(r0, size), :] = (gate * _sigmoid(gate) * up).astype(hb_ref.dtype)

        _for_row_groups(rows, sub)

    _zero_tail(hb_ref, rows)


def _moe_down_kernel(sb_exp, sb_rows, hb_ref, wd_ref, y_ref):
    rows = sb_rows[pl.program_id(0)]

    @pl.when(rows > 0)
    def _compute():
        wd = wd_ref[0].astype(BF16)

        def sub(r0, size):
            y_ref[pl.ds(r0, size), :] = _dot(hb_ref[pl.ds(r0, size), :], wd)

        _for_row_groups(rows, sub)

    _zero_tail(y_ref, rows)


def _moe_ffn(hn, sb_exp, sb_rows, row_tok, w_gate, w_up, w_down):
    d = hn.shape[1]
    fdim = w_gate.shape[2]
    rmax = MOE_ROWS
    n_sb = sb_exp.shape[0]
    fc = min(fdim, MOE_HIDDEN_CHUNK)
    nf = fdim // fc

    def chunk(r, s, j, last):
        return jnp.where(r[s] > 0, j, last)

    up_spec = pltpu.PrefetchScalarGridSpec(
        num_scalar_prefetch=3,
        grid=(n_sb, nf),
        in_specs=[pl.BlockSpec(memory_space=pl.ANY),
                  pl.BlockSpec((1, d, fc), lambda s, f, e, r, t: (e[s], 0, chunk(r, s, f, nf - 1))),
                  pl.BlockSpec((1, d, fc), lambda s, f, e, r, t: (e[s], 0, chunk(r, s, f, nf - 1)))],
        out_specs=pl.BlockSpec((rmax, fc), lambda s, f, e, r, t: (s, f)),
        scratch_shapes=[pltpu.VMEM((2, MOE_GRAN, d), F32),
                        pltpu.VMEM((rmax, d), BF16),
                        pltpu.VMEM((d, fc), BF16),
                        pltpu.VMEM((d, fc), BF16),
                        pltpu.SemaphoreType.DMA((2,))],
    )
    hb = pl.pallas_call(
        functools.partial(_moe_up_kernel, rmax=rmax),
        grid_spec=up_spec,
        out_shape=jax.ShapeDtypeStruct((n_sb * rmax, fdim), BF16),
        compiler_params=_cparams("arbitrary", "arbitrary"),
        name="moe_up",
    )(sb_exp, sb_rows, row_tok, hn, w_gate, w_up)
    tn = _col_tile(d)
    nn = d // tn
    down_spec = pltpu.PrefetchScalarGridSpec(
        num_scalar_prefetch=2,
        grid=(n_sb, nn),
        in_specs=[pl.BlockSpec((rmax, fdim), lambda s, j, e, r: (s, 0)),
                  pl.BlockSpec((1, fdim, tn), lambda s, j, e, r: (e[s], 0, chunk(r, s, j, nn - 1)))],
        out_specs=pl.BlockSpec((rmax, tn), lambda s, j, e, r: (s, j)),
    )
    return pl.pallas_call(
        _moe_down_kernel,
        grid_spec=down_spec,
        out_shape=jax.ShapeDtypeStruct((n_sb * rmax, d), F32),
        compiler_params=_cparams("arbitrary", "arbitrary"),
        name="moe_down",
    )(sb_exp, sb_rows, hb, w_down)


def _combine_kernel(pos_ref, h_ref, gate_ref, g_ref, yb_hbm, yp_ref, ys_ref, buf_ref, sem, *, nb_p):
    i = pl.program_id(0)
    tb = h_ref.shape[0]
    slot = i % 2

    def issue_block(blk, slot):
        def issue(t, _):
            for k in range(TOP_K):
                row = pos_ref[(blk * tb + t) * TOP_K + k]
                pltpu.make_async_copy(yb_hbm.at[pl.ds(row, 1), :],
                                      buf_ref.at[slot, k, pl.ds(t, 1), :], sem.at[slot]).start()
            return 0

        lax.fori_loop(0, tb, issue, 0, unroll=4)

    @pl.when(i == 0)
    def _():
        issue_block(0, 0)

    @pl.when(i + 1 < pl.num_programs(0))
    def _():
        issue_block(i + 1, 1 - slot)

    for k in range(TOP_K):
        pltpu.make_async_copy(yb_hbm.at[pl.ds(0, tb), :], buf_ref.at[slot, k], sem.at[slot]).wait()
    gate = gate_ref[...]
    y = h_ref[...] + gate[:, 0:1] * buf_ref[slot, 0] + gate[:, 1:2] * buf_ref[slot, 1]
    y = (y * lax.rsqrt(jnp.mean(y * y, axis=-1, keepdims=True) + EPS)) * g_ref[...]

    @pl.when(i < nb_p)
    def _():
        yp_ref[...] = y

    @pl.when(i == nb_p)
    def _():
        ys_ref[...] = y


def _combine(pos, h, gate, g_final, yb, m_p):
    m, d = h.shape
    tb = m - m_p
    assert m_p % tb == 0 and tb % SUBLANES == 0
    nb_p = m_p // tb
    grid_spec = pltpu.PrefetchScalarGridSpec(
        num_scalar_prefetch=1,
        grid=(nb_p + 1,),
        in_specs=[pl.BlockSpec((tb, d), lambda i, p: (i, 0)),
                  pl.BlockSpec((tb, LANES), lambda i, p: (i, 0)),
                  pl.BlockSpec((1, d), lambda i, p: (0, 0)),
                  pl.BlockSpec(memory_space=pl.ANY)],
        out_specs=[pl.BlockSpec((tb, d), lambda i, p: (jnp.minimum(i, nb_p - 1), 0)),
                   pl.BlockSpec((tb, d), lambda i, p: (0, 0))],
        scratch_shapes=[pltpu.VMEM((2, TOP_K, tb, d), F32), pltpu.SemaphoreType.DMA((2,))],
    )
    return pl.pallas_call(
        functools.partial(_combine_kernel, nb_p=nb_p),
        grid_spec=grid_spec,
        out_shape=[jax.ShapeDtypeStruct((m_p, d), F32), jax.ShapeDtypeStruct((tb, d), F32)],
        compiler_params=_cparams("arbitrary"),
        name="combine",
    )(pos, h, gate, g_final.reshape(1, d), yb)


def _dispatch_plan(eid, n_tok):
    a = n_tok * TOP_K
    rmax, gran = MOE_ROWS, MOE_GRAN
    n_sb = -(-a // rmax) + N_EXPERTS
    e_flat = eid.reshape(a)
    onehot = (e_flat[:, None] == jnp.arange(N_EXPERTS, dtype=I32)[None, :]).astype(I32)
    csum = jnp.cumsum(onehot, axis=0)
    counts = csum[-1]
    rank = jnp.take_along_axis(csum, e_flat[:, None], axis=1)[:, 0] - 1
    padded = (counts + gran - 1) // gran * gran
    slabs = (padded + rmax - 1) // rmax
    slab_end = jnp.cumsum(slabs)
    slab_start = slab_end - slabs
    total = slab_end[-1]
    pos = slab_start[e_flat] * rmax + rank
    sidx = jnp.arange(n_sb, dtype=I32)
    s_eff = jnp.minimum(sidx, jnp.maximum(total - 1, 0))
    exp_of = jnp.minimum(jnp.searchsorted(slab_end, s_eff, side="right"), N_EXPERTS - 1).astype(I32)
    local = s_eff - slab_start[exp_of]
    rows = jnp.clip(padded[exp_of] - local * rmax, 0, rmax)
    rows = jnp.where(sidx < total, rows, 0).astype(I32)
    tok = jnp.repeat(jnp.arange(n_tok, dtype=I32), TOP_K)
    row_tok = jnp.zeros((n_sb * rmax,), I32).at[pos].set(tok)
    return exp_of, rows, row_tok, pos.astype(I32)


def kernel(x_prompt, x_sample, cache_kv0, cache_kv1, cache_kv2, state_conv, state_rglru, norm1, w_in,
           conv_w, conv_b, lru_wa, lru_ba, lru_wx, lru_bx, lru_lambda, w_branch_a, w_branch_b, w_out,
           norm2, router_group_w, router_group_b, router_expert_w, router_expert_b, expert_w_gate,
           expert_w_up, expert_w_down, norm_final):
    batch, seq, d = x_prompt.shape
    n_dec, dec_seq, _ = x_sample.shape
    assert w_in.shape[0] == 1 and dec_seq == 1
    hpg, e = HEADS_PER_GROUP, HEAD_DIM
    d_att = N_GROUPS * hpg * e
    q_off, k_off, v_off = 2 * d, 2 * d + d_att, 2 * d + 2 * d_att
    ga_off, gb_off = 2 * d + 3 * d_att, 3 * d + 3 * d_att
    caches = (cache_kv0[0], cache_kv1[0], cache_kv2[0])
    for c, (win, dil) in zip(caches, DIL_GROUPS):
        assert c.shape[1] == win == SPAN * dil and seq % (SPAN * dil) == 0
    m_p = batch * seq
    n_tok = m_p + n_dec
    slopes = 2.0 ** (-8.0 * jnp.arange(1, N_GROUPS * hpg + 1, dtype=F32) / (N_GROUPS * hpg))
    n_pad = LANES - N_EXPERT_GROUPS - N_EXPERTS
    w_router = jnp.concatenate(
        [router_group_w[0], router_expert_w[0], jnp.zeros((d, n_pad), F32)], axis=1).astype(BF16)
    b_router = jnp.concatenate(
        [router_group_b[0], router_expert_b[0], jnp.zeros((n_pad,), F32)]).reshape(1, LANES)
    rnn_w = (conv_w[0], conv_b[0], lru_wa[0], lru_ba[0], lru_wx[0], lru_bx[0], lru_lambda[0])

    x_all = jnp.concatenate([x_prompt.reshape(m_p, d), x_sample.reshape(n_dec, d)], axis=0)
    xn = _rmsnorm(x_all, norm1[0], BF16)
    proj, kv_wide = _matmul_and_shift(xn, _to_bf16(w_in[0]), caches[-1])
    proj_s = proj[m_p:]

    a_p, ph = _rglru_prompt(proj, batch, seq, d, *rnn_w)
    o_p = _attn_prompt(proj, slopes, batch, seq, q_off, k_off, v_off)

    conv_t = jnp.transpose(state_conv[0], (1, 0, 2))
    a_s, new_conv_t, new_h = _rglru_step(proj, m_p, n_dec, conv_t, state_rglru[0], d, *rnn_w)
    heads = lambda off: proj_s[:, off:off + d_att].reshape(n_dec, N_GROUPS, hpg, e)
    q_s, k_s, v_s = heads(q_off), heads(k_off), heads(v_off)
    slopes_b = jnp.broadcast_to(slopes.reshape(N_GROUPS, hpg, 1), (N_GROUPS, hpg, LANES))
    o_s = _attn_decode(q_s, k_s, v_s, slopes_b, caches).reshape(n_dec, hpg * e)
    kv_new = [jnp.stack([k_s[:, g], v_s[:, g]], axis=1) for g in range(N_GROUPS)]
    s_kv = [_cache_shift(c, new) for c, new in zip(caches[:-1], kv_new[:-1])]
    s_kv.append(_set_last_row(kv_wide, kv_new[-1]))

    a_all = jnp.concatenate([a_p, a_s], axis=0)
    o_all = jnp.concatenate([o_p, o_s], axis=0)
    u = _merge(a_all, o_all, w_branch_a[0], w_branch_b[0], proj, ga_off, gb_off)
    h = _outproj(x_all, u, w_out[0])

    hn, eid, gate = _norm_router(h, norm2[0], w_router, b_router)
    sb_exp, sb_rows, row_tok, pos = _dispatch_plan(eid[:, :TOP_K], n_tok)
    yb = _moe_ffn(hn, sb_exp, sb_rows, row_tok, expert_w_gate[0], expert_w_up[0], expert_w_down[0])
    y_p, y_s = _combine(pos, h, gate, norm_final, yb, m_p)

    def heads_p(off, g):
        cols = proj[:m_p, off + g * hpg * e:off + (g + 1) * hpg * e]
        return cols.reshape(batch, seq, hpg, e)

    p_kv = []
    for g, (win, _) in enumerate(DIL_GROUPS):
        w = min(win, seq)
        p_kv.append(jnp.stack([heads_p(k_off, g)[:, -w:], heads_p(v_off, g)[:, -w:]], axis=2)[None])
    p_conv = proj[:m_p, :d].reshape(batch, seq, d)[:, -(CONV_W - 1):][None]
    p_h = ph.reshape(batch, d)[None]
    s_conv = jnp.transpose(new_conv_t, (1, 0, 2))[None]
    return (y_p.reshape(batch, seq, d), y_s.reshape(n_dec, 1, d), p_kv[0], p_kv[1], p_kv[2],
            p_conv, p_h, s_kv[0][None], s_kv[1][None], s_kv[2][None], s_conv, new_h[None])
```

```python
import functools

import jax
import jax.numpy as jnp
from jax import lax
from jax.experimental import pallas as pl
from jax.experimental.pallas import tpu as pltpu

F32, BF16, I32 = jnp.float32, jnp.bfloat16, jnp.int32

N_RNN_BLOCKS = 16
CONV_W = 4
LRU_C = 8.0
HEAD_DIM = 128
HEADS_PER_GROUP = 8
DIL_GROUPS = ((128, 1), (512, 4), (2048, 16))
N_GROUPS = len(DIL_GROUPS)
SPAN = 128
N_EXPERT_GROUPS = 4
EXPERTS_PER_GROUP = 8
N_EXPERTS = N_EXPERT_GROUPS * EXPERTS_PER_GROUP
TOP_K = 2
EPS = 1e-6
PAST_LEN = 2048
ATT_SCALE = HEAD_DIM ** -0.5
NEG = -1e30

LANES = 128
SUBLANES = 8
VMEM_LIMIT = 56 * 1024 * 1024
ROW_TILE = 1024
COL_TILE = 512
MOE_ROWS = 1024
MOE_GRAN = 128
MOE_HIDDEN_CHUNK = 256
SCAN_CHUNKS = 16
SCAN_PITCH_PAD = 8
CACHE_ROWS = 512
MM_COPY_ROWS = 256
MM_COPY_CHUNKS = 4
CAST_BLOCK_BYTES = 10 * 1024 * 1024
ATTN_UNROLL = 4


def _cparams(*sem):
    return pltpu.CompilerParams(dimension_semantics=sem, vmem_limit_bytes=VMEM_LIMIT)


def _bf16_round(x):
    return x.astype(BF16).astype(F32)


def _sigmoid(x):
    return 1.0 / (1.0 + jnp.exp(-x))


def _gelu_tanh(x):
    return 0.5 * x * (1.0 + jnp.tanh(0.7978845608028654 * (x + 0.044715 * (x * x * x))))


def _one_minus_exp2x(x):
    t = jnp.tanh(x)
    return (-2.0 * t) / (1.0 - t)


def _softplus(x):
    return jnp.maximum(x, 0.0) + jnp.log1p(jnp.exp(-jnp.abs(x)))


def _dot(a, b):
    return jnp.dot(a, b, preferred_element_type=F32)


def _col_tile(n, *offsets):
    t = COL_TILE
    while t > LANES and (n % t or any(o % t for o in offsets)):
        t //= 2
    return t


def _row_tile(m, pref=ROW_TILE):
    best = None
    for t in range(LANES, min(m, pref) + 1, LANES):
        if m % t == 0:
            best = t
    return best or m


def _rmsnorm_kernel(x_ref, g_ref, o_ref):
    x = x_ref[...]
    y = x * lax.rsqrt(jnp.mean(x * x, axis=-1, keepdims=True) + EPS)
    o_ref[...] = (y * g_ref[...]).astype(o_ref.dtype)


def _rmsnorm(x, g, out_dtype):
    m, d = x.shape
    tm = _row_tile(m, 256)
    return pl.pallas_call(
        _rmsnorm_kernel,
        grid=(m // tm,),
        in_specs=[pl.BlockSpec((tm, d), lambda i: (i, 0)),
                  pl.BlockSpec((1, d), lambda i: (0, 0))],
        out_specs=pl.BlockSpec((tm, d), lambda i: (i, 0)),
        out_shape=jax.ShapeDtypeStruct((m, d), out_dtype),
        compiler_params=_cparams("parallel"),
        name="rmsnorm",
    )(x, g.reshape(1, d))


def _cast_kernel(w_ref, o_ref):
    o_ref[...] = w_ref[...].astype(o_ref.dtype)


def _to_bf16(w):
    k, n = w.shape
    tk = _row_tile(k)
    tn = max(t for t in range(LANES, n + 1, LANES) if n % t == 0 and tk * t * 4 <= CAST_BLOCK_BYTES)
    return pl.pallas_call(
        _cast_kernel,
        grid=(k // tk, n // tn),
        in_specs=[pl.BlockSpec((tk, tn), lambda i, j: (i, j))],
        out_specs=pl.BlockSpec((tk, tn), lambda i, j: (i, j)),
        out_shape=jax.ShapeDtypeStruct((k, n), BF16),
        compiler_params=_cparams("parallel", "parallel"),
        name="to_bf16",
    )(w)


def _mm_copy_kernel(*refs, plan, cps, rows):
    n_c = len(plan)
    x_ref, w_ref = refs[:2]
    caches = refs[2:2 + n_c]
    o_ref = refs[2 + n_c]
    couts = refs[3 + n_c:3 + 2 * n_c]
    ring_ref, sem_in, sem_out = refs[3 + 2 * n_c:]
    t = pl.program_id(0) * pl.num_programs(1) + pl.program_id(1)
    copy_steps = plan[-1][0] + plan[-1][1]

    def chunks(cache_hbm, cout_hbm, step, slot, phase):
        w_rows = cache_hbm.shape[1]
        cpn = w_rows // rows
        for q in range(cps):
            c = step * cps + q
            n = c // cpn
            r0 = (c % cpn) * rows
            buf = ring_ref.at[slot, q]
            if phase in ("read", "write"):
                body = pltpu.make_async_copy(cache_hbm.at[n, pl.ds(r0 + 1, rows - 1)],
                                             buf.at[pl.ds(0, rows - 1)], sem_in.at[slot, q])
                tail = pltpu.make_async_copy(
                    cache_hbm.at[n, pl.ds(jnp.minimum(r0 + rows, w_rows - 1), 1)],
                    buf.at[pl.ds(rows - 1, 1)], sem_in.at[slot, q])
            if phase in ("write", "done"):
                dst = pltpu.make_async_copy(buf, cout_hbm.at[n, pl.ds(r0, rows)], sem_out.at[slot, q])
            if phase == "read":
                body.start()
                tail.start()
            elif phase == "write":
                body.wait()
                tail.wait()
                dst.start()
            else:
                dst.wait()

    def for_chunks(step, phase):
        for (first, count), cache_hbm, cout_hbm in zip(plan, caches, couts):
            @pl.when((step >= first) & (step < first + count))
            def _(first=first, cache_hbm=cache_hbm, cout_hbm=cout_hbm):
                chunks(cache_hbm, cout_hbm, step - first, step % 2, phase)

    @pl.when((t >= 2) & (t < copy_steps + 2))
    def _():
        for_chunks(t - 2, "done")

    @pl.when(t < copy_steps)
    def _():
        for_chunks(t, "read")

    @pl.when((t >= 1) & (t < copy_steps + 1))
    def _():
        for_chunks(t - 1, "write")

    o_ref[...] = _dot(x_ref[...], w_ref[...]).astype(o_ref.dtype)


def _copy_steps(cache):
    n_seq, w_rows = cache.shape[:2]
    n_chunks = n_seq * (w_rows // MM_COPY_ROWS)
    if w_rows % MM_COPY_ROWS or n_chunks % MM_COPY_CHUNKS:
        return None
    return n_chunks // MM_COPY_CHUNKS


def _mm_tiles(m, n):
    return _row_tile(m), (2 * COL_TILE if n % (2 * COL_TILE) == 0 else _col_tile(n))


def _plan_shift(m, n, caches):
    tm, tn = _mm_tiles(m, n)
    budget = (n // tn) * (m // tm) - 2
    chosen = []
    for g in sorted(range(len(caches)), key=lambda g: -caches[g].shape[1]):
        steps = _copy_steps(caches[g])
        if steps is not None and steps <= budget:
            chosen.append(g)
            budget -= steps
    return chosen


def _matmul_and_shift(x, w, caches):
    m, k = x.shape
    n = w.shape[1]
    tm, tn = _mm_tiles(m, n)
    plan, first = [], 0
    for c in caches:
        plan.append((first, _copy_steps(c)))
        first += plan[-1][1]
    assert first + 2 <= (n // tn) * (m // tm)
    any_spec = pl.BlockSpec(memory_space=pl.ANY)
    outs = pl.pallas_call(
        functools.partial(_mm_copy_kernel, plan=tuple(plan), cps=MM_COPY_CHUNKS, rows=MM_COPY_ROWS),
        grid=(n // tn, m // tm),
        in_specs=[pl.BlockSpec((tm, k), lambda j, i: (i, 0)),
                  pl.BlockSpec((k, tn), lambda j, i: (0, j))] + [any_spec] * len(caches),
        out_specs=[pl.BlockSpec((tm, tn), lambda j, i: (i, j))] + [any_spec] * len(caches),
        out_shape=[jax.ShapeDtypeStruct((m, n), F32)]
        + [jax.ShapeDtypeStruct(c.shape, c.dtype) for c in caches],
        scratch_shapes=[pltpu.VMEM((2, MM_COPY_CHUNKS, MM_COPY_ROWS) + caches[0].shape[2:],
                                   caches[0].dtype),
                        pltpu.SemaphoreType.DMA((2, MM_COPY_CHUNKS)),
                        pltpu.SemaphoreType.DMA((2, MM_COPY_CHUNKS))],
        compiler_params=_cparams("arbitrary", "arbitrary"),
        name="matmul_shift",
    )(x, w, *caches)
    return outs[0], list(outs[1:])


def _set_last_row_kernel(c_hbm, new_ref, o_hbm, sem):
    del c_hbm
    cp = pltpu.make_async_copy(new_ref, o_hbm.at[:, o_hbm.shape[1] - 1], sem)
    cp.start()
    cp.wait()


def _set_last_row(cache, new):
    return pl.pallas_call(
        _set_last_row_kernel,
        in_specs=[pl.BlockSpec(memory_space=pl.ANY), pl.BlockSpec(memory_space=pltpu.VMEM)],
        out_specs=pl.BlockSpec(memory_space=pl.ANY),
        out_shape=jax.ShapeDtypeStruct(cache.shape, cache.dtype),
        scratch_shapes=[pltpu.SemaphoreType.DMA(())],
        input_output_aliases={0: 0},
        name="set_last_row",
    )(cache, new)


def _merge_kernel(a_ref, o_ref, wa_ref, wb_ref, ga_ref, gb_ref, u_ref, wab_ref, wbb_ref):
    @pl.when(pl.program_id(1) == 0)
    def _():
        wab_ref[...] = wa_ref[...].astype(BF16)
        wbb_ref[...] = wb_ref[...].astype(BF16)

    ya = _dot(a_ref[...], wab_ref[...])
    yb = _dot(o_ref[...], wbb_ref[...])
    u = _sigmoid(ga_ref[...]) * ya + _sigmoid(gb_ref[...]) * yb
    u_ref[...] = u.astype(u_ref.dtype)


def _merge(a, o, w_a, w_b, proj, ga_off, gb_off):
    m, ka = a.shape
    kb = o.shape[1]
    n = w_a.shape[1]
    tm = _row_tile(m)
    tn = _col_tile(n, ga_off, gb_off)
    return pl.pallas_call(
        _merge_kernel,
        grid=(n // tn, m // tm),
        in_specs=[pl.BlockSpec((tm, ka), lambda j, i: (i, 0)),
                  pl.BlockSpec((tm, kb), lambda j, i: (i, 0)),
                  pl.BlockSpec((ka, tn), lambda j, i: (0, j)),
                  pl.BlockSpec((kb, tn), lambda j, i: (0, j)),
                  pl.BlockSpec((tm, tn), lambda j, i: (i, j + ga_off // tn)),
                  pl.BlockSpec((tm, tn), lambda j, i: (i, j + gb_off // tn))],
        out_specs=pl.BlockSpec((tm, tn), lambda j, i: (i, j)),
        out_shape=jax.ShapeDtypeStruct((m, n), BF16),
        scratch_shapes=[pltpu.VMEM((ka, tn), BF16), pltpu.VMEM((kb, tn), BF16)],
        compiler_params=_cparams("parallel", "arbitrary"),
        name="merge",
    )(a, o, w_a, w_b, proj, proj)


def _outproj_kernel(x_ref, u_ref, w_ref, h_ref, wb_ref):
    @pl.when(pl.program_id(1) == 0)
    def _():
        wb_ref[...] = w_ref[...].astype(BF16)

    h_ref[...] = x_ref[...] + _dot(u_ref[...], wb_ref[...])


def _outproj(x, u, w):
    m, k = u.shape
    n = w.shape[1]
    tm = _row_tile(m)
    tn = _col_tile(n)
    return pl.pallas_call(
        _outproj_kernel,
        grid=(n // tn, m // tm),
        in_specs=[pl.BlockSpec((tm, tn), lambda j, i: (i, j)),
                  pl.BlockSpec((tm, k), lambda j, i: (i, 0)),
                  pl.BlockSpec((k, tn), lambda j, i: (0, j))],
        out_specs=pl.BlockSpec((tm, tn), lambda j, i: (i, j)),
        out_shape=jax.ShapeDtypeStruct((m, n), F32),
        scratch_shapes=[pltpu.VMEM((k, tn), BF16)],
        compiler_params=_cparams("parallel", "arbitrary"),
        name="outproj",
    )(x, u, w)


def _rglru_prompt_kernel(xr_ref, yr_ref, cw_ref, cb_ref, wa_ref, ba_ref, wx_ref, bx_ref, lam_ref,
                         a_out_ref, ph_ref,
                         xe_ref, a_ref, u_ref, hs_ref, as_ref, hsz_ref, *, t_len, nc):
    c = xr_ref.shape[1]
    nl = c // LANES
    cl = t_len // nc
    pitch = cl + SCAN_PITCH_PAD
    xe_ref[0:SUBLANES, :] = jnp.zeros((SUBLANES, c), F32)
    xe_ref[SUBLANES:SUBLANES + t_len, :] = _bf16_round(xr_ref[...])
    wa = wa_ref[0].astype(BF16)
    wx = wx_ref[0].astype(BF16)
    decay = (-LRU_C) * _softplus(-lam_ref[...])
    cw = _bf16_round(cw_ref[...])
    for ch in range(nc):
        r0 = ch * cl
        xc = cw[0:1] * xe_ref[r0 + 5:r0 + 5 + cl, :]
        for j in range(1, CONV_W):
            xc = xc + cw[j:j + 1] * xe_ref[r0 + 5 + j:r0 + 5 + j + cl, :]
        xc = xc + cb_ref[...]
        xcb = xc.astype(BF16)
        r = _sigmoid(_dot(xcb, wa) + ba_ref[...])
        i = _sigmoid(_dot(xcb, wx) + bx_ref[...])
        log_a = r * decay
        a = jnp.exp(log_a)
        mult = jnp.sqrt(_one_minus_exp2x(log_a))
        if ch == 0:
            row = lax.broadcasted_iota(I32, (cl, c), 0)
            a = jnp.where(row == 0, 0.0, a)
            mult = jnp.where(row == 0, 1.0, mult)
        au = (a, mult * (i * xc))
        for lb in range(nl):
            for ref, val in zip((a_ref, u_ref), au):
                ref[lb, ch * pitch:ch * pitch + cl, :] = val[:, lb * LANES:(lb + 1) * LANES]

    def pass1(j, carry):
        out = []
        for lb in range(nl):
            acc_a, acc_h = carry[2 * lb], carry[2 * lb + 1]
            a = a_ref[lb, pl.ds(j, nc, stride=pitch), :]
            u = u_ref[lb, pl.ds(j, nc, stride=pitch), :]
            acc_a = a * acc_a
            acc_h = a * acc_h + u
            as_ref[lb, j] = acc_a
            hsz_ref[lb, j] = acc_h
            out += [acc_a, acc_h]
        return tuple(out)

    init = (jnp.ones((nc, LANES), F32), jnp.zeros((nc, LANES), F32)) * nl
    fin = lax.fori_loop(0, cl, pass1, init)
    cins = []
    for lb in range(nl):
        fin_a, fin_h = fin[2 * lb], fin[2 * lb + 1]
        carry = jnp.zeros((1, LANES), F32)
        rows = []
        for ch in range(nc):
            rows.append(carry)
            carry = fin_a[ch:ch + 1] * carry + fin_h[ch:ch + 1]
        cins.append(jnp.concatenate(rows, axis=0))
        ph_ref[0, :, lb * LANES:(lb + 1) * LANES] = carry

    def pass2(j, _):
        for lb in range(nl):
            hs_ref[lb, pl.ds(j, nc, stride=pitch), :] = hsz_ref[lb, j] + as_ref[lb, j] * cins[lb]
        return 0

    lax.fori_loop(0, cl, pass2, 0)
    for ch in range(nc):
        for lb in range(nl):
            y = yr_ref[ch * cl:(ch + 1) * cl, lb * LANES:(lb + 1) * LANES]
            a_out_ref[ch * cl:(ch + 1) * cl, lb * LANES:(lb + 1) * LANES] = (
                _gelu_tanh(y) * hs_ref[lb, ch * pitch:ch * pitch + cl, :]).astype(a_out_ref.dtype)


def _rglru_prompt(proj, batch, t_len, d, conv_w, conv_b, lru_wa, lru_ba, lru_wx, lru_bx, lam):
    nb = lru_wa.shape[0]
    c = d // nb
    nl = c // LANES
    nc = SCAN_CHUNKS
    cl = t_len // nc
    pitch = cl + SCAN_PITCH_PAD
    vec = lambda v: v.reshape(1, d)
    vspec = pl.BlockSpec((1, c), lambda b, n: (0, n))
    wspec = pl.BlockSpec((1, c, c), lambda b, n: (n, 0, 0))
    return pl.pallas_call(
        functools.partial(_rglru_prompt_kernel, t_len=t_len, nc=nc),
        grid=(batch, nb),
        in_specs=[pl.BlockSpec((t_len, c), lambda b, n: (b, n)),
                  pl.BlockSpec((t_len, c), lambda b, n: (b, n + nb)),
                  pl.BlockSpec((CONV_W, c), lambda b, n: (0, n)),
                  vspec, wspec, vspec, wspec, vspec, vspec],
        out_specs=[pl.BlockSpec((t_len, c), lambda b, n: (b, n)),
                   pl.BlockSpec((1, 1, c), lambda b, n: (b, 0, n))],
        out_shape=[jax.ShapeDtypeStruct((batch * t_len, d), BF16),
                   jax.ShapeDtypeStruct((batch, 1, d), F32)],
        scratch_shapes=[pltpu.VMEM((t_len + SUBLANES, c), F32),
                        pltpu.VMEM((nl, nc * pitch, LANES), F32),
                        pltpu.VMEM((nl, nc * pitch, LANES), F32),
                        pltpu.VMEM((nl, nc * pitch, LANES), F32),
                        pltpu.VMEM((nl, cl, nc, LANES), F32),
                        pltpu.VMEM((nl, cl, nc, LANES), F32)],
        compiler_params=_cparams("parallel", "parallel"),
        name="rglru_prompt",
    )(proj, proj, conv_w, vec(conv_b), lru_wa, vec(lru_ba), lru_wx, vec(lru_bx), vec(lam))


def _rglru_step_kernel(xr_ref, yr_ref, cs_ref, h0_ref, cw_ref, cb_ref, wa_ref, ba_ref, wx_ref, bx_ref,
                       lam_ref, a_out_ref, ncs_ref, nh_ref, *, reset):
    x = xr_ref[...]
    cw = cw_ref[...]
    xc = cw[0:1] * cs_ref[0]
    for j in range(1, CONV_W - 1):
        xc = xc + cw[j:j + 1] * cs_ref[j]
    xc = xc + cw[CONV_W - 1:CONV_W] * x + cb_ref[...]
    xcb = xc.astype(BF16)
    r = _sigmoid(_dot(xcb, wa_ref[0].astype(BF16)) + ba_ref[...])
    i = _sigmoid(_dot(xcb, wx_ref[0].astype(BF16)) + bx_ref[...])
    log_a = r * ((-LRU_C) * _softplus(-lam_ref[...]))
    if reset:
        h = i * xc
    else:
        h = jnp.exp(log_a) * h0_ref[...] + jnp.sqrt(_one_minus_exp2x(log_a)) * (i * xc)
    nh_ref[...] = h
    a_out_ref[...] = (_gelu_tanh(yr_ref[...]) * h).astype(a_out_ref.dtype)
    for j in range(CONV_W - 2):
        ncs_ref[j] = cs_ref[j + 1]
    ncs_ref[CONV_W - 2] = x


def _rglru_step(proj, row_off, n, conv_state, h0, d, conv_w, conv_b, lru_wa, lru_ba, lru_wx, lru_bx,
                lam):
    nb = lru_wa.shape[0]
    c = d // nb
    assert row_off % n == 0
    rb = row_off // n
    vec = lambda v: v.reshape(1, d)
    vspec = pl.BlockSpec((1, c), lambda j: (0, j))
    wspec = pl.BlockSpec((1, c, c), lambda j: (j, 0, 0))
    xspec = pl.BlockSpec((n, c), lambda j: (0, j))
    sspec = pl.BlockSpec((CONV_W - 1, n, c), lambda j: (0, 0, j))
    return pl.pallas_call(
        functools.partial(_rglru_step_kernel, reset=(PAST_LEN == 0)),
        grid=(nb,),
        in_specs=[pl.BlockSpec((n, c), lambda j: (rb, j)),
                  pl.BlockSpec((n, c), lambda j: (rb, j + nb)), sspec, xspec,
                  pl.BlockSpec((CONV_W, c), lambda j: (0, j)),
                  vspec, wspec, vspec, wspec, vspec, vspec],
        out_specs=[xspec, sspec, xspec],
        out_shape=[jax.ShapeDtypeStruct((n, d), BF16),
                   jax.ShapeDtypeStruct((CONV_W - 1, n, d), F32),
                   jax.ShapeDtypeStruct((n, d), F32)],
        compiler_params=_cparams("parallel"),
        name="rglru_step",
    )(proj, proj, conv_state, h0, conv_w, vec(conv_b), lru_wa, vec(lru_ba), lru_wx, vec(lru_bx),
      vec(lam))


def _unroll_for(trips):
    return max(u for u in range(1, ATTN_UNROLL + 1) if trips % u == 0)


def _merge_groups(lses, outs):
    m = functools.reduce(jnp.maximum, lses)
    es = [jnp.exp(x - m) for x in lses]
    den = functools.reduce(lambda a, b: a + b, es)
    acc = _bf16_round(es[0] / den) * _bf16_round(outs[0])
    for g in range(1, len(outs)):
        acc = acc + _bf16_round(es[g] / den) * _bf16_round(outs[g])
    return acc


def _attn_prompt_kernel(slope_ref, *refs, t_len):
    qkv = refs[:3 * N_GROUPS]
    o_ref = refs[3 * N_GROUPS]
    og_ref, lse_ref = refs[3 * N_GROUPS + 1:]
    h = pl.program_id(1)

    for g, (_, dil) in enumerate(DIL_GROUPS):
        q_ref, k_ref, v_ref = qkv[3 * g:3 * g + 3]
        slope = slope_ref[g * HEADS_PER_GROUP + h] * float(dil)
        n_blk = t_len // dil // SPAN

        def rows(ref, start, size, dil=dil):
            if dil == 1:
                return ref[pl.ds(start, size), :]
            return ref[pl.ds(start, size, stride=dil), :]

        def put(ref, g, start, val, dil=dil):
            if dil == 1:
                ref[g, pl.ds(start, SPAN), :] = val
            else:
                ref[g, pl.ds(start, SPAN, stride=dil), :] = val

        def block(base, with_prev, g=g, dil=dil, slope=slope, q_ref=q_ref, k_ref=k_ref,
                  v_ref=v_ref, rows=rows, put=put):
            nk = 2 * SPAN if with_prev else SPAN
            k0 = base - SPAN * dil if with_prev else base
            q = (rows(q_ref, base, SPAN) * ATT_SCALE).astype(BF16)
            kk = rows(k_ref, k0, nk).astype(BF16)
            vv = rows(v_ref, k0, nk).astype(BF16)
            s = lax.dot_general(q, kk, (((1,), (1,)), ((), ())), preferred_element_type=F32)
            qi = lax.broadcasted_iota(I32, (SPAN, nk), 0)
            kj = lax.broadcasted_iota(I32, (SPAN, nk), 1)
            diff = qi - kj + (SPAN if with_prev else 0)
            valid = (diff >= 0) & (diff <= SPAN)
            s = jnp.where(valid, s - slope * diff.astype(F32), NEG)
            m = jnp.max(s, axis=-1, keepdims=True)
            p = jnp.exp(s - m)
            l = jnp.sum(p, axis=-1, keepdims=True)
            put(og_ref, g, base, _dot((p / l).astype(BF16), vv))
            put(lse_ref, g, base, jnp.broadcast_to(m + jnp.log(l), (SPAN, LANES)))

        def per_class(r, _, block=block, n_blk=n_blk, dil=dil):
            block(r, False)

            def later(n, _):
                block(n * (SPAN * dil) + r, True)
                return 0

            if n_blk > 1:
                lax.fori_loop(1, n_blk, later, 0, unroll=_unroll_for(n_blk - 1))
            return 0

        if dil == 1:
            per_class(0, 0)
        else:
            lax.fori_loop(0, dil, per_class, 0, unroll=_unroll_for(dil) if n_blk == 1 else 1)

    rc = 256
    for c0 in range(0, t_len, rc):
        acc = _merge_groups([lse_ref[g, c0:c0 + rc, :] for g in range(N_GROUPS)],
                            [og_ref[g, c0:c0 + rc, :] for g in range(N_GROUPS)])
        o_ref[c0:c0 + rc, :] = acc.astype(o_ref.dtype)


def _attn_prompt(proj, slopes, batch, t_len, q_off, k_off, v_off):
    hpg, e = HEADS_PER_GROUP, HEAD_DIM
    in_specs = []
    for g in range(N_GROUPS):
        for off in (q_off, k_off, v_off):
            cb = off // e + g * hpg
            in_specs.append(pl.BlockSpec((t_len, e), lambda b, h, s, cb=cb: (b, cb + h)))
    grid_spec = pltpu.PrefetchScalarGridSpec(
        num_scalar_prefetch=1,
        grid=(batch, hpg),
        in_specs=in_specs,
        out_specs=pl.BlockSpec((t_len, e), lambda b, h, s: (b, h)),
        scratch_shapes=[pltpu.VMEM((N_GROUPS, t_len, e), F32),
                        pltpu.VMEM((N_GROUPS, t_len, LANES), F32)],
    )
    return pl.pallas_call(
        functools.partial(_attn_prompt_kernel, t_len=t_len),
        grid_spec=grid_spec,
        out_shape=jax.ShapeDtypeStruct((batch * t_len, hpg * e), BF16),
        compiler_params=_cparams("parallel", "parallel"),
        name="attn_prompt",
    )(slopes, *([proj] * (3 * N_GROUPS)))


def _attn_decode_kernel(q_ref, kn_ref, vn_ref, sl_ref, c0_ref, c1_ref, c2_ref, o_ref):
    caches = (c0_ref, c1_ref, c2_ref)
    outs, lses = [], []
    for g, (_, dil) in enumerate(DIL_GROUPS):
        cache = caches[g]
        q = _bf16_round(q_ref[0, g] * ATT_SCALE)
        kk = _bf16_round(cache[0, :, 0])
        vv = _bf16_round(cache[0, :, 1])
        slope = sl_ref[g][:, 0:1]
        s = jnp.sum(kk * q[None], axis=-1, keepdims=True)
        j = lax.broadcasted_iota(I32, s.shape, 0)
        s = s - slope[None] * ((SPAN - j) * dil).astype(F32)
        s_new = jnp.sum(_bf16_round(kn_ref[0, g]) * q, axis=-1, keepdims=True)
        m = jnp.maximum(jnp.max(s, axis=0), s_new)
        p = jnp.exp(s - m[None])
        p_new = jnp.exp(s_new - m)
        l = jnp.sum(p, axis=0) + p_new
        o = (jnp.sum(_bf16_round(p / l[None]) * vv, axis=0)
             + _bf16_round(p_new / l) * _bf16_round(vn_ref[0, g]))
        outs.append(o)
        lses.append(m + jnp.log(l))
    o_ref[0] = _merge_groups(lses, outs).astype(o_ref.dtype)


def _attn_decode(q, k_new, v_new, slopes_b, caches):
    n = q.shape[0]
    hpg, e = HEADS_PER_GROUP, HEAD_DIM
    qspec = pl.BlockSpec((1, N_GROUPS, hpg, e), lambda i: (i, 0, 0, 0))
    views, cspecs = [], []
    for g, (_, dil) in enumerate(DIL_GROUPS):
        views.append(caches[g].reshape(n, SPAN, dil, 2, hpg, e))
        cspecs.append(pl.BlockSpec((1, SPAN, None, 2, hpg, e), lambda i: (i, 0, 0, 0, 0, 0)))
    return pl.pallas_call(
        _attn_decode_kernel,
        grid=(n,),
        in_specs=[qspec, qspec, qspec,
                  pl.BlockSpec((N_GROUPS, hpg, LANES), lambda i: (0, 0, 0))] + cspecs,
        out_specs=pl.BlockSpec((1, hpg, e), lambda i: (i, 0, 0)),
        out_shape=jax.ShapeDtypeStruct((n, hpg, e), BF16),
        compiler_params=_cparams("parallel"),
        name="attn_decode",
    )(q, k_new, v_new, slopes_b, *views)


def _cache_shift_kernel(c_ref, nxt_ref, new_ref, o_ref):
    r = c_ref.shape[1]
    o_ref[0, 0:r - 1] = c_ref[0, 1:r]
    last = pl.program_id(1) == pl.num_programs(1) - 1
    o_ref[0, r - 1] = jnp.where(last, new_ref[0], nxt_ref[0, 0])


def _cache_shift(cache, new):
    n, w = cache.shape[:2]
    tail = cache.shape[2:]
    r = min(w, CACHE_ROWS)
    zeros = (0,) * len(tail)
    return pl.pallas_call(
        _cache_shift_kernel,
        grid=(n, w // r),
        in_specs=[pl.BlockSpec((1, r) + tail, lambda b, i: (b, i) + zeros),
                  pl.BlockSpec((1, 1) + tail, lambda b, i: (b, jnp.minimum((i + 1) * r, w - 1)) + zeros),
                  pl.BlockSpec((1,) + tail, lambda b, i: (b,) + zeros)],
        out_specs=pl.BlockSpec((1, r) + tail, lambda b, i: (b, i) + zeros),
        out_shape=jax.ShapeDtypeStruct(cache.shape, cache.dtype),
        compiler_params=_cparams("parallel", "parallel"),
        name="cache_shift",
    )(cache, cache, new)


def _norm_router_kernel(h_ref, g_ref, wr_ref, br_ref, hn_ref, eid_ref, gate_ref):
    h = h_ref[...]
    hn = (h * lax.rsqrt(jnp.mean(h * h, axis=-1, keepdims=True) + EPS)) * g_ref[...]
    hn_ref[...] = hn
    logits = _dot(hn.astype(BF16), wr_ref[...]) + br_ref[...]
    lane = lax.broadcasted_iota(I32, logits.shape, 1)
    ng, epg = N_EXPERT_GROUPS, EXPERTS_PER_GROUP

    def first_argmax(vals, mask):
        v = jnp.where(mask, vals, NEG)
        vmax = jnp.max(v, axis=-1, keepdims=True)
        idx = jnp.min(jnp.where(mask & (v == vmax), lane, 4 * LANES), axis=-1, keepdims=True)
        return vmax, idx

    gmask = lane < ng
    gmax, gsel = first_argmax(logits, gmask)
    p_group = 1.0 / jnp.sum(jnp.where(gmask, jnp.exp(logits - gmax), 0.0), axis=-1, keepdims=True)
    lo = ng + gsel * epg
    emask = (lane >= lo) & (lane < lo + epg)
    v1, i1 = first_argmax(logits, emask)
    v2, i2 = first_argmax(logits, emask & (lane != i1))
    e2 = jnp.exp(v2 - v1)
    g1 = p_group / (1.0 + e2)
    g2 = p_group * e2 / (1.0 + e2)
    eid_ref[...] = jnp.where(lane == 0, i1 - ng, jnp.where(lane == 1, i2 - ng, 0))
    gate_ref[...] = jnp.where(lane == 0, g1, jnp.where(lane == 1, g2, 0.0))


def _norm_router(h, g, w_router, b_router):
    m, d = h.shape
    tm = _row_tile(m, 256)
    row = lambda i: (i, 0)
    fixed = lambda i: (0, 0)
    return pl.pallas_call(
        _norm_router_kernel,
        grid=(m // tm,),
        in_specs=[pl.BlockSpec((tm, d), row),
                  pl.BlockSpec((1, d), fixed),
                  pl.BlockSpec((d, LANES), fixed),
                  pl.BlockSpec((1, LANES), fixed)],
        out_specs=[pl.BlockSpec((tm, d), row),
                   pl.BlockSpec((tm, LANES), row),
                   pl.BlockSpec((tm, LANES), row)],
        out_shape=[jax.ShapeDtypeStruct((m, d), F32),
                   jax.ShapeDtypeStruct((m, LANES), I32),
                   jax.ShapeDtypeStruct((m, LANES), F32)],
        compiler_params=_cparams("parallel"),
        name="norm_router",
    )(h, g.reshape(1, d), w_router, b_router)


def _for_row_groups(rows, fn):
    big = 2 * MOE_GRAN

    def pair(i, _):
        fn(pl.multiple_of(i * big, big), big)
        return 0

    n_pair = rows // big
    lax.fori_loop(0, n_pair, pair, 0)

    @pl.when(rows % big != 0)
    def _():
        fn(pl.multiple_of(n_pair * big, big), MOE_GRAN)


def _zero_tail(ref, rows):
    def fill(i, _):
        ref[pl.ds(pl.multiple_of(i * MOE_GRAN, MOE_GRAN), MOE_GRAN), :] = jnp.zeros(
            (MOE_GRAN, ref.shape[1]), ref.dtype)
        return 0

    lax.fori_loop(rows // MOE_GRAN, ref.shape[0] // MOE_GRAN, fill, 0)


def _moe_up_kernel(sb_exp, sb_rows, row_tok, x_hbm, wg_ref, wu_ref, hb_ref,
                   stage_ref, xb_ref, wgb_ref, wub_ref, sem, *, rmax):
    s = pl.program_id(0)
    f = pl.program_id(1)
    rows = sb_rows[s]
    gran = MOE_GRAN

    @pl.when((rows > 0) & (f == 0))
    def _gather():
        n_groups = rows // gran

        def issue_group(gi, slot):
            base = s * rmax + gi * gran

            def issue(i, _):
                pltpu.make_async_copy(x_hbm.at[pl.ds(row_tok[base + i], 1), :],
                                      stage_ref.at[slot, pl.ds(i, 1), :], sem.at[slot]).start()
                return 0

            lax.fori_loop(0, gran, issue, 0, unroll=8)

        issue_group(0, 0)

        def group(gi, _):
            slot = gi % 2

            @pl.when(gi + 1 < n_groups)
            def _():
                issue_group(gi + 1, 1 - slot)

            pltpu.make_async_copy(x_hbm.at[pl.ds(0, gran), :], stage_ref.at[slot],
                                  sem.at[slot]).wait()
            xb_ref[pl.ds(pl.multiple_of(gi * gran, gran), gran), :] = stage_ref[slot].astype(BF16)
            return 0

        lax.fori_loop(0, n_groups, group, 0)

    @pl.when(rows > 0)
    def _compute():
        wgb_ref[...] = wg_ref[0].astype(BF16)
        wub_ref[...] = wu_ref[0].astype(BF16)

        def sub(r0, size):
            x = xb_ref[pl.ds(r0, size), :]
            gate = _dot(x, wgb_ref[...])
            up = _dot(x, wub_ref[...])
            hb_ref[pl.ds(r0, size), :] = (gate * _sigmoid(gate) * up).astype(hb_ref.dtype)

        _for_row_groups(rows, sub)

    _zero_tail(hb_ref, rows)


def _moe_down_kernel(sb_exp, sb_rows, hb_ref, wd_ref, y_ref):
    rows = sb_rows[pl.program_id(0)]

    @pl.when(rows > 0)
    def _compute():
        wd = wd_ref[0].astype(BF16)

        def sub(r0, size):
            y_ref[pl.ds(r0, size), :] = _dot(hb_ref[pl.ds(r0, size), :], wd)

        _for_row_groups(rows, sub)

    _zero_tail(y_ref, rows)


def _moe_ffn(hn, sb_exp, sb_rows, row_tok, w_gate, w_up, w_down):
    d = hn.shape[1]
    fdim = w_gate.shape[2]
    rmax = MOE_ROWS
    n_sb = sb_exp.shape[0]
    fc = min(fdim, MOE_HIDDEN_CHUNK)
    nf = fdim // fc

    def chunk(r, s, j, last):
        return jnp.where(r[s] > 0, j, last)

    up_spec = pltpu.PrefetchScalarGridSpec(
        num_scalar_prefetch=3,
        grid=(n_sb, nf),
        in_specs=[pl.BlockSpec(memory_space=pl.ANY),
                  pl.BlockSpec((1, d, fc), lambda s, f, e, r, t: (e[s], 0, chunk(r, s, f, nf - 1))),
                  pl.BlockSpec((1, d, fc), lambda s, f, e, r, t: (e[s], 0, chunk(r, s, f, nf - 1)))],
        out_specs=pl.BlockSpec((rmax, fc), lambda s, f, e, r, t: (s, f)),
        scratch_shapes=[pltpu.VMEM((2, MOE_GRAN, d), F32),
                        pltpu.VMEM((rmax, d), BF16),
                        pltpu.VMEM((d, fc), BF16),
                        pltpu.VMEM((d, fc), BF16),
                        pltpu.SemaphoreType.DMA((2,))],
    )
    hb = pl.pallas_call(
        functools.partial(_moe_up_kernel, rmax=rmax),
        grid_spec=up_spec,
        out_shape=jax.ShapeDtypeStruct((n_sb * rmax, fdim), BF16),
        compiler_params=_cparams("arbitrary", "arbitrary"),
        name="moe_up",
    )(sb_exp, sb_rows, row_tok, hn, w_gate, w_up)
    tn = _col_tile(d)
    nn = d // tn
    down_spec = pltpu.PrefetchScalarGridSpec(
        num_scalar_prefetch=2,
        grid=(n_sb, nn),
        in_specs=[pl.BlockSpec((rmax, fdim), lambda s, j, e, r: (s, 0)),
                  pl.BlockSpec((1, fdim, tn), lambda s, j, e, r: (e[s], 0, chunk(r, s, j, nn - 1)))],
        out_specs=pl.BlockSpec((rmax, tn), lambda s, j, e, r: (s, j)),
    )
    return pl.pallas_call(
        _moe_down_kernel,
        grid_spec=down_spec,
        out_shape=jax.ShapeDtypeStruct((n_sb * rmax, d), F32),
        compiler_params=_cparams("arbitrary", "arbitrary"),
        name="moe_down",
    )(sb_exp, sb_rows, hb, w_down)


def _combine_kernel(pos_ref, h_ref, gate_ref, g_ref, yb_hbm, yp_ref, ys_ref, buf_ref, sem, *, nb_p):
    i = pl.program_id(0)
    tb = h_ref.shape[0]
    slot = i % 2

    def issue_block(blk, slot):
        def issue(t, _):
            for k in range(TOP_K):
                row = pos_ref[(blk * tb + t) * TOP_K + k]
                pltpu.make_async_copy(yb_hbm.at[pl.ds(row, 1), :],
                                      buf_ref.at[slot, k, pl.ds(t, 1), :], sem.at[slot]).start()
            return 0

        lax.fori_loop(0, tb, issue, 0, unroll=4)

    @pl.when(i == 0)
    def _():
        issue_block(0, 0)

    @pl.when(i + 1 < pl.num_programs(0))
    def _():
        issue_block(i + 1, 1 - slot)

    for k in range(TOP_K):
        pltpu.make_async_copy(yb_hbm.at[pl.ds(0, tb), :], buf_ref.at[slot, k], sem.at[slot]).wait()
    gate = gate_ref[...]
    y = h_ref[...] + gate[:, 0:1] * buf_ref[slot, 0] + gate[:, 1:2] * buf_ref[slot, 1]
    y = (y * lax.rsqrt(jnp.mean(y * y, axis=-1, keepdims=True) + EPS)) * g_ref[...]

    @pl.when(i < nb_p)
    def _():
        yp_ref[...] = y

    @pl.when(i == nb_p)
    def _():
        ys_ref[...] = y


def _combine(pos, h, gate, g_final, yb, m_p):
    m, d = h.shape
    tb = m - m_p
    assert m_p % tb == 0 and tb % SUBLANES == 0
    nb_p = m_p // tb
    grid_spec = pltpu.PrefetchScalarGridSpec(
        num_scalar_prefetch=1,
        grid=(nb_p + 1,),
        in_specs=[pl.BlockSpec((tb, d), lambda i, p: (i, 0)),
                  pl.BlockSpec((tb, LANES), lambda i, p: (i, 0)),
                  pl.BlockSpec((1, d), lambda i, p: (0, 0)),
                  pl.BlockSpec(memory_space=pl.ANY)],
        out_specs=[pl.BlockSpec((tb, d), lambda i, p: (jnp.minimum(i, nb_p - 1), 0)),
                   pl.BlockSpec((tb, d), lambda i, p: (0, 0))],
        scratch_shapes=[pltpu.VMEM((2, TOP_K, tb, d), F32), pltpu.SemaphoreType.DMA((2,))],
    )
    return pl.pallas_call(
        functools.partial(_combine_kernel, nb_p=nb_p),
        grid_spec=grid_spec,
        out_shape=[jax.ShapeDtypeStruct((m_p, d), F32), jax.ShapeDtypeStruct((tb, d), F32)],
        compiler_params=_cparams("arbitrary"),
        name="combine",
    )(pos, h, gate, g_final.reshape(1, d), yb)


def _dispatch_plan(eid, n_tok):
    a = n_tok * TOP_K
    rmax, gran = MOE_ROWS, MOE_GRAN
    n_sb = -(-a // rmax) + N_EXPERTS
    e_flat = eid.reshape(a)
    onehot = (e_flat[:, None] == jnp.arange(N_EXPERTS, dtype=I32)[None, :]).astype(I32)
    csum = jnp.cumsum(onehot, axis=0)
    counts = csum[-1]
    rank = jnp.take_along_axis(csum, e_flat[:, None], axis=1)[:, 0] - 1
    padded = (counts + gran - 1) // gran * gran
    slabs = (padded + rmax - 1) // rmax
    slab_end = jnp.cumsum(slabs)
    slab_start = slab_end - slabs
    total = slab_end[-1]
    pos = slab_start[e_flat] * rmax + rank
    sidx = jnp.arange(n_sb, dtype=I32)
    s_eff = jnp.minimum(sidx, jnp.maximum(total - 1, 0))
    exp_of = jnp.minimum(jnp.searchsorted(slab_end, s_eff, side="right"), N_EXPERTS - 1).astype(I32)
    local = s_eff - slab_start[exp_of]
    rows = jnp.clip(padded[exp_of] - local * rmax, 0, rmax)
    rows = jnp.where(sidx < total, rows, 0).astype(I32)
    tok = jnp.repeat(jnp.arange(n_tok, dtype=I32), TOP_K)
    row_tok = jnp.zeros((n_sb * rmax,), I32).at[pos].set(tok)
    return exp_of, rows, row_tok, pos.astype(I32)


def kernel(x_prompt, x_sample, cache_kv0, cache_kv1, cache_kv2, state_conv, state_rglru, norm1, w_in,
           conv_w, conv_b, lru_wa, lru_ba, lru_wx, lru_bx, lru_lambda, w_branch_a, w_branch_b, w_out,
           norm2, router_group_w, router_group_b, router_expert_w, router_expert_b, expert_w_gate,
           expert_w_up, expert_w_down, norm_final):
    batch, seq, d = x_prompt.shape
    n_dec, dec_seq, _ = x_sample.shape
    assert w_in.shape[0] == 1 and dec_seq == 1
    hpg, e = HEADS_PER_GROUP, HEAD_DIM
    d_att = N_GROUPS * hpg * e
    q_off, k_off, v_off = 2 * d, 2 * d + d_att, 2 * d + 2 * d_att
    ga_off, gb_off = 2 * d + 3 * d_att, 3 * d + 3 * d_att
    caches = (cache_kv0[0], cache_kv1[0], cache_kv2[0])
    for c, (win, dil) in zip(caches, DIL_GROUPS):
        assert c.shape[1] == win == SPAN * dil and seq % (SPAN * dil) == 0
    m_p = batch * seq
    n_tok = m_p + n_dec
    slopes = 2.0 ** (-8.0 * jnp.arange(1, N_GROUPS * hpg + 1, dtype=F32) / (N_GROUPS * hpg))
    n_pad = LANES - N_EXPERT_GROUPS - N_EXPERTS
    w_router = jnp.concatenate(
        [router_group_w[0], router_expert_w[0], jnp.zeros((d, n_pad), F32)], axis=1).astype(BF16)
    b_router = jnp.concatenate(
        [router_group_b[0], router_expert_b[0], jnp.zeros((n_pad,), F32)]).reshape(1, LANES)
    rnn_w = (conv_w[0], conv_b[0], lru_wa[0], lru_ba[0], lru_wx[0], lru_bx[0], lru_lambda[0])

    x_all = jnp.concatenate([x_prompt.reshape(m_p, d), x_sample.reshape(n_dec, d)], axis=0)
    xn = _rmsnorm(x_all, norm1[0], BF16)
    riding = _plan_shift(n_tok, w_in.shape[2], caches)
    proj, shifted = _matmul_and_shift(xn, _to_bf16(w_in[0]), [caches[g] for g in riding])
    proj_s = proj[m_p:]

    a_p, ph = _rglru_prompt(proj, batch, seq, d, *rnn_w)
    o_p = _attn_prompt(proj, slopes, batch, seq, q_off, k_off, v_off)

    conv_t = jnp.transpose(state_conv[0], (1, 0, 2))
    a_s, new_conv_t, new_h = _rglru_step(proj, m_p, n_dec, conv_t, state_rglru[0], d, *rnn_w)
    heads = lambda off: proj_s[:, off:off + d_att].reshape(n_dec, N_GROUPS, hpg, e)
    q_s, k_s, v_s = heads(q_off), heads(k_off), heads(v_off)
    slopes_b = jnp.broadcast_to(slopes.reshape(N_GROUPS, hpg, 1), (N_GROUPS, hpg, LANES))
    o_s = _attn_decode(q_s, k_s, v_s, slopes_b, caches).reshape(n_dec, hpg * e)
    kv_new = [jnp.stack([k_s[:, g], v_s[:, g]], axis=1) for g in range(N_GROUPS)]
    s_kv = [_set_last_row(shifted[riding.index(g)], kv_new[g]) if g in riding
            else _cache_shift(caches[g], kv_new[g]) for g in range(N_GROUPS)]

    a_all = jnp.concatenate([a_p, a_s], axis=0)
    o_all = jnp.concatenate([o_p, o_s], axis=0)
    u = _merge(a_all, o_all, w_branch_a[0], w_branch_b[0], proj, ga_off, gb_off)
    h = _outproj(x_all, u, w_out[0])

    hn, eid, gate = _norm_router(h, norm2[0], w_router, b_router)
    sb_exp, sb_rows, row_tok, pos = _dispatch_plan(eid[:, :TOP_K], n_tok)
    yb = _moe_ffn(hn, sb_exp, sb_rows, row_tok, expert_w_gate[0], expert_w_up[0], expert_w_down[0])
    y_p, y_s = _combine(pos, h, gate, norm_final, yb, m_p)

    def last_rows(col0, width, w):
        return jnp.stack([lax.slice(proj, ((b + 1) * seq - w, col0), ((b + 1) * seq, col0 + width))
                          for b in range(batch)])

    p_kv = []
    for g, (win, _) in enumerate(DIL_GROUPS):
        w = min(win, seq)
        kv = [last_rows(off + g * hpg * e, hpg * e, w).reshape(batch, w, hpg, e)
              for off in (k_off, v_off)]
        p_kv.append(jnp.stack(kv, axis=2)[None])
    p_conv = last_rows(0, d, CONV_W - 1)[None]
    p_h = ph.reshape(batch, d)[None]
    s_conv = jnp.transpose(new_conv_t, (1, 0, 2))[None]
    return (y_p.reshape(batch, seq, d), y_s.reshape(n_dec, 1, d), p_kv[0], p_kv[1], p_kv[2],
            p_conv, p_h, s_kv[0][None], s_kv[1][None], s_kv[2][None], s_conv, new_h[None])
```
